```python
import jax, jax.numpy as jnp
from jax import lax
import numpy as np

D_MODEL = 1024
BATCH = 32
SEQ = 256
DEPTH = 4
DEC_BATCH = 8
DEC_SEQ = 4096
PAST_LEN = 512

GRID_W = 64
N_MLA_LAYERS = (DEPTH + 1) // 2
N_POOL_LAYERS = DEPTH // 2
N_MOD = 9
FFN_HIDDEN = 2816
EPS = 1e-6
MLA_HEADS = D_MODEL // 128
Q_LORA = D_MODEL // 4
KV_LORA = D_MODEL // 8
QK_NOPE = 64
QK_ROPE = 32
V_HEAD = 64
QK_HEAD = QK_NOPE + QK_ROPE
ROPE_BASE = 10000.0
ATTN_BLOCK = 128
MLA_WIDTH = MLA_HEADS * V_HEAD
GMLP_GROUPS = 4
GMLP_CHUNK = 128
GMLP_WIDTH = D_MODEL // 2
GMLP_GROUP_CH = GMLP_WIDTH // GMLP_GROUPS
MIX_WIDTH = MLA_WIDTH + GMLP_WIDTH
OFF_KV = Q_LORA
OFF_KR = Q_LORA + KV_LORA
OFF_G = Q_LORA + KV_LORA + QK_ROPE
IN_DIM = OFF_G + 2 * GMLP_WIDTH
POOL_WINDOWS = (2, 4, 8, 16)
POOL_GROUP_CH = D_MODEL // len(POOL_WINDOWS)

kernel_name = 'hybrid_mla_gmlp_pool_diffusion_step'


def rms_norm(x, g):
    x32 = x.astype(jnp.float32)
    y = x32 * lax.rsqrt(jnp.mean(x32 * x32, axis=-1, keepdims=True) + EPS)
    return (y * g.astype(jnp.float32)).astype(x.dtype)


def modulate(x, g, m, k):
    return rms_norm(x, g) * (1 + m[:, 3 * k + 1, None, :]) + m[:, 3 * k, None, :]


def swiglu(h, w1, w3, w2):
    return (jax.nn.silu(h @ w1) * (h @ w3)) @ w2


def axial_rope_tables(rows):
    row = jnp.repeat(jnp.arange(rows), GRID_W).astype(jnp.float32)
    col = jnp.tile(jnp.arange(GRID_W), rows).astype(jnp.float32)
    per_axis = QK_ROPE // 2
    inv = ROPE_BASE ** (-jnp.arange(0, per_axis, 2, dtype=jnp.float32) / per_axis)
    ang = jnp.concatenate([row[:, None] * inv, col[:, None] * inv], axis=-1)
    return jnp.cos(ang), jnp.sin(ang)


def apply_rope(x, cos, sin):
    half = QK_ROPE // 4
    xr = x.astype(jnp.float32).reshape(x.shape[:-1] + (2, 2, half))
    x1, x2 = xr[..., 0, :], xr[..., 1, :]
    c = cos.reshape(cos.shape[0], 2, half)[None, :, None]
    s = sin.reshape(sin.shape[0], 2, half)[None, :, None]
    out = jnp.stack([x1 * c - x2 * s, x1 * s + x2 * c], axis=-2)
    return out.reshape(x.shape).astype(x.dtype)


def block_attention(q, k, v):
    B, Lq, H, Dh = q.shape
    nblk = Lq // ATTN_BLOCK
    qb = q.reshape(B, nblk, ATTN_BLOCK, H, Dh).transpose(1, 0, 2, 3, 4)
    scale = Dh ** -0.5

    def one(qblk):
        s = jnp.einsum('bqhd,bkhd->bhqk', qblk, k, preferred_element_type=jnp.float32) * scale
        p = jax.nn.softmax(s, axis=-1).astype(v.dtype)
        return jnp.einsum('bhqk,bkhd->bqhd', p, v)

    o = lax.map(one, qb)
    return o.transpose(1, 0, 2, 3, 4).reshape(B, Lq, H, v.shape[-1])


def mla_kv(kv_lat, k_rope, w_kvb, k_norm):
    B, L, _ = kv_lat.shape
    kv = (kv_lat @ w_kvb).reshape(B, L, MLA_HEADS, QK_NOPE + V_HEAD)
    k_nope, v = kv[..., :QK_NOPE], kv[..., QK_NOPE:]
    k = jnp.concatenate([k_nope, jnp.broadcast_to(k_rope[:, :, None, :], (B, L, MLA_HEADS, QK_ROPE))], axis=-1)
    return rms_norm(k, k_norm), v


def chunk_gmlp(g, v_norm, ws, bs):
    u, v = g[..., :GMLP_WIDTH], g[..., GMLP_WIDTH:]
    v = rms_norm(v, v_norm)
    B, L, _ = v.shape
    vr = v.reshape(B, L // GMLP_CHUNK, GMLP_CHUNK, GMLP_GROUPS, GMLP_GROUP_CH)
    mixed = jnp.einsum('gpq,bnqgc->bnpgc', ws, vr) + bs.T[:, :, None]
    return u * mixed.reshape(B, L, GMLP_WIDTH)


def multiscale_pool(h, pool_w, pool_scale):
    B, L, D = h.shape
    h32 = h.astype(jnp.float32)
    cs = jnp.concatenate([jnp.zeros((B, 1, D), jnp.float32), jnp.cumsum(h32, axis=1)], axis=1)
    t = jnp.arange(L)
    outs = []
    for gi, w in enumerate(POOL_WINDOWS):
        sl = slice(gi * POOL_GROUP_CH, (gi + 1) * POOL_GROUP_CH)
        lo = jnp.clip(t - w // 2, 0, L)
        hi = jnp.clip(t + w - w // 2, 0, L)
        csg = cs[:, :, sl]
        mean = (csg[:, hi] - csg[:, lo]) / (hi - lo).astype(jnp.float32)[None, :, None]
        outs.append((mean - h32[:, :, sl]).astype(h.dtype) @ pool_w[gi])
    return jnp.concatenate(outs, axis=-1) * pool_scale


def trunk(x, cvec, rope, cache_ckv, cache_krope, w_mod, b_mod, norm_g, ffn_w1, ffn_w3, ffn_w2,
          w_in, q_a_norm, kv_a_norm, w_qb, w_kvb, q_norm, k_norm, gmlp_v_norm, gmlp_ws, gmlp_b,
          w_out, pool_w, pool_scale):
    B, L, _ = x.shape
    ckv_out, krope_out = [], []
    for i in range(DEPTH):
        m = (jax.nn.silu(cvec) @ w_mod[i] + b_mod[i]).reshape(cvec.shape[0], N_MOD, D_MODEL)
        h = modulate(x, norm_g[i, 0], m, 0)
        x = x + 0.5 * m[:, 2, None, :] * swiglu(h, ffn_w1[i, 0], ffn_w3[i, 0], ffn_w2[i, 0])
        h = modulate(x, norm_g[i, 1], m, 1)
        if i % 2 == 0:
            j = i // 2
            proj = h @ w_in[j]
            q_lat = rms_norm(proj[..., :OFF_KV], q_a_norm[j])
            kv_lat = rms_norm(proj[..., OFF_KV:OFF_KR], kv_a_norm[j])
            k_rope = proj[..., OFF_KR:OFF_G]
            gm = jax.nn.gelu(proj[..., OFF_G:])
            q = rms_norm((q_lat @ w_qb[j]).reshape(B, L, MLA_HEADS, QK_HEAD), q_norm[j])
            k, v = mla_kv(kv_lat, k_rope, w_kvb[j], k_norm[j])
            if cache_ckv is None:
                attn = block_attention(q, k, v)
                ckv_out.append(kv_lat)
                krope_out.append(k_rope)
            else:
                cos, sin = rope
                q = jnp.concatenate([q[..., :QK_NOPE], apply_rope(q[..., QK_NOPE:], cos, sin)], axis=-1)
                k = jnp.concatenate([k[..., :QK_NOPE], apply_rope(k[..., QK_NOPE:], cos, sin)], axis=-1)
                kc, vc = mla_kv(cache_ckv[:, j], cache_krope[:, j], w_kvb[j], k_norm[j])
                attn = block_attention(q, jnp.concatenate([k, kc], axis=1), jnp.concatenate([v, vc], axis=1))
            gout = chunk_gmlp(gm, gmlp_v_norm[j], gmlp_ws[j], gmlp_b[j])
            mix = jnp.concatenate([attn.reshape(B, L, MLA_WIDTH), gout], axis=-1) @ w_out[j]
        else:
            j = i // 2
            mix = multiscale_pool(h, pool_w[j], pool_scale[j])
        x = x + m[:, 5, None, :] * mix
        h = modulate(x, norm_g[i, 2], m, 2)
        x = x + 0.5 * m[:, 8, None, :] * swiglu(h, ffn_w1[i, 1], ffn_w3[i, 1], ffn_w2[i, 1])
    return x, ckv_out, krope_out


def setup_inputs(seed: int = 0) -> dict:
    key = jax.random.key(seed)
    ks = jax.random.split(key, 32)
    nrm = lambda k, shape, s: jax.random.normal(k, shape, jnp.float32) * s
    gain = lambda k, shape: 1.0 + 0.02 * jax.random.normal(k, shape, jnp.float32)
    return {
        'x_prompt': nrm(ks[0], (BATCH, SEQ, D_MODEL), 1.0),
        'x_sample': nrm(ks[1], (DEC_BATCH, DEC_SEQ, D_MODEL), 1.0),
        'cache_ckv': nrm(ks[2], (DEC_BATCH, N_MLA_LAYERS, PAST_LEN, KV_LORA), 1.0),
        'cache_krope': nrm(ks[3], (DEC_BATCH, N_MLA_LAYERS, PAST_LEN, QK_ROPE), 1.0),
        'c': nrm(ks[4], (DEC_BATCH, D_MODEL), 1.0),
        'c_ctx': nrm(ks[5], (D_MODEL,), 1.0),
        'w_mod': nrm(ks[6], (DEPTH, D_MODEL, N_MOD * D_MODEL), 0.5 * D_MODEL ** -0.5),
        'b_mod': nrm(ks[7], (DEPTH, N_MOD * D_MODEL), 0.02),
        'norm_g': gain(ks[8], (DEPTH, 3, D_MODEL)),
        'ffn_w1': nrm(ks[9], (DEPTH, 2, D_MODEL, FFN_HIDDEN), D_MODEL ** -0.5),
        'ffn_w3': nrm(ks[10], (DEPTH, 2, D_MODEL, FFN_HIDDEN), D_MODEL ** -0.5),
        'ffn_w2': nrm(ks[11], (DEPTH, 2, FFN_HIDDEN, D_MODEL), FFN_HIDDEN ** -0.5),
        'w_in': nrm(ks[12], (N_MLA_LAYERS, D_MODEL, IN_DIM), D_MODEL ** -0.5),
        'q_a_norm': gain(ks[13], (N_MLA_LAYERS, Q_LORA)),
        'kv_a_norm': gain(ks[14], (N_MLA_LAYERS, KV_LORA)),
        'w_qb': nrm(ks[15], (N_MLA_LAYERS, Q_LORA, MLA_HEADS * QK_HEAD), Q_LORA ** -0.5),
        'w_kvb': nrm(ks[16], (N_MLA_LAYERS, KV_LORA, MLA_HEADS * (QK_NOPE + V_HEAD)), KV_LORA ** -0.5),
        'q_norm': gain(ks[17], (N_MLA_LAYERS, QK_HEAD)),
        'k_norm': gain(ks[18], (N_MLA_LAYERS, QK_HEAD)),
        'gmlp_v_norm': gain(ks[19], (N_MLA_LAYERS, GMLP_WIDTH)),
        'gmlp_ws': nrm(ks[20], (N_MLA_LAYERS, GMLP_GROUPS, GMLP_CHUNK, GMLP_CHUNK), GMLP_CHUNK ** -0.5),
        'gmlp_b': nrm(ks[21], (N_MLA_LAYERS, GMLP_GROUPS, GMLP_CHUNK), 0.02),
        'w_out': nrm(ks[22], (N_MLA_LAYERS, MIX_WIDTH, D_MODEL), MIX_WIDTH ** -0.5),
        'pool_w': nrm(ks[23], (N_POOL_LAYERS, len(POOL_WINDOWS), POOL_GROUP_CH, POOL_GROUP_CH), POOL_GROUP_CH ** -0.5),
        'pool_scale': gain(ks[24], (N_POOL_LAYERS, D_MODEL)),
    }


def reference(x_prompt, x_sample, cache_ckv, cache_krope, c, c_ctx, w_mod, b_mod, norm_g,
              ffn_w1, ffn_w3, ffn_w2, w_in, q_a_norm, kv_a_norm, w_qb, w_kvb, q_norm, k_norm,
              gmlp_v_norm, gmlp_ws, gmlp_b, w_out, pool_w, pool_scale):
    weights = (w_mod, b_mod, norm_g, ffn_w1, ffn_w3, ffn_w2, w_in, q_a_norm, kv_a_norm, w_qb,
               w_kvb, q_norm, k_norm, gmlp_v_norm, gmlp_ws, gmlp_b, w_out, pool_w, pool_scale)
    y_prompt, ckv_list, krope_list = trunk(x_prompt, c_ctx[None, :], None, None, None, *weights)
    new_ckv = jnp.stack(ckv_list, axis=1)
    new_krope = jnp.stack(krope_list, axis=1)
    ROWS = x_sample.shape[1] // GRID_W
    rope = axial_rope_tables(ROWS)
    y_sample, _, _ = trunk(x_sample, c, rope, cache_ckv, cache_krope, *weights)
    return (y_prompt, y_sample, new_ckv, new_krope)
```

```python
import functools

import jax
import jax.numpy as jnp
from jax import lax
from jax.experimental import pallas as pl
from jax.experimental.pallas import tpu as pltpu

D_MODEL = 1024
DEPTH = 4
N_MOD = 9
FFN_HIDDEN = 2816
EPS = 1e-6
MLA_HEADS = 8
Q_LORA = 256
KV_LORA = 128
QK_NOPE = 64
QK_ROPE = 32
V_HEAD = 64
QK_HEAD = QK_NOPE + QK_ROPE
GRID_W = 64
ROPE_BASE = 10000.0
GMLP_GROUPS = 4
GMLP_CHUNK = 128
GMLP_WIDTH = 512
POOL_WINDOWS = (2, 4, 8, 16)
POOL_GROUP_CH = D_MODEL // len(POOL_WINDOWS)
POOL_HALO = max(POOL_WINDOWS) // 2

LANES = 128
SUBLANES = 8
MXU_TILE = 256
HEAD_W = MLA_HEADS * LANES
PROJ_W = Q_LORA + KV_LORA + LANES + 2 * GMLP_WIDTH
VMEM_LIMIT = 56 * 1024 * 1024

BF16 = jnp.bfloat16
F32 = jnp.float32


def _cparams(n_axes):
    return pltpu.CompilerParams(
        dimension_semantics=("arbitrary",) * n_axes, vmem_limit_bytes=VMEM_LIMIT)


def _rms(x, width):
    ss = jnp.sum(x * x, axis=-1, keepdims=True) * (1.0 / width)
    return x * lax.rsqrt(ss + EPS)


def _modulated(x, g_ref, mod_ref, k):
    shift = mod_ref[0, 3 * k:3 * k + 1, :]
    scale = mod_ref[0, 3 * k + 1:3 * k + 2, :]
    return (_rms(x, D_MODEL) * g_ref[...]) * (1.0 + scale) + shift


def _dot(a, b):
    return jnp.dot(a, b, preferred_element_type=F32)


def _dot_nt(a, b):
    return lax.dot_general(a, b, (((1,), (1,)), ((), ())), preferred_element_type=F32)


def _mod_kernel(c_ref, w_ref, b_ref, o_ref):
    c = c_ref[...]
    a = (c * (1.0 / (1.0 + jnp.exp(-c)))).astype(BF16)
    o_ref[0] = _dot(a, w_ref[0].astype(BF16)) + b_ref[0]


def _modulation(cvecs, w_mod, b_mod):
    rows = cvecs.shape[0]
    tn = D_MODEL
    n_out = N_MOD * D_MODEL
    return pl.pallas_call(
        _mod_kernel,
        grid=(DEPTH, n_out // tn),
        in_specs=[
            pl.BlockSpec((rows, D_MODEL), lambda i, j: (0, 0)),
            pl.BlockSpec((1, D_MODEL, tn), lambda i, j: (i, 0, j)),
            pl.BlockSpec((1, 1, tn), lambda i, j: (i, 0, j)),
        ],
        out_specs=pl.BlockSpec((1, rows, tn), lambda i, j: (i, 0, j)),
        out_shape=jax.ShapeDtypeStruct((DEPTH, rows, n_out), F32),
        compiler_params=_cparams(2),
        name="modulation",
    )(cvecs, w_mod, b_mod.reshape(DEPTH, 1, n_out))


def _ffn_kernel(x_ref, mod_ref, g_ref, w1_ref, w3_ref, w2_ref, o_ref, *, k):
    x = x_ref[...]
    hb = _modulated(x, g_ref, mod_ref, k).astype(BF16)
    acc = jnp.zeros(x.shape, F32)
    for c in range(FFN_HIDDEN // MXU_TILE):
        cols = slice(c * MXU_TILE, (c + 1) * MXU_TILE)
        a = _dot(hb, w1_ref[:, cols])
        b = _dot(hb, w3_ref[:, cols])
        gated = (a * (1.0 / (1.0 + jnp.exp(-a))) * b).astype(BF16)
        acc = acc + _dot(gated, w2_ref[cols, :])
    gate = mod_ref[0, 3 * k + 2:3 * k + 3, :]
    o_ref[...] = x + (0.5 * gate) * acc


def _ffn(x, mod, g, w1, w3, w2, *, k, seq, tm):
    t = x.shape[0]
    per_seq = seq // tm
    const = lambda i: (0, 0)
    return pl.pallas_call(
        functools.partial(_ffn_kernel, k=k),
        grid=(t // tm,),
        in_specs=[
            pl.BlockSpec((tm, D_MODEL), lambda i: (i, 0)),
            pl.BlockSpec((1, N_MOD, D_MODEL), lambda i: (i // per_seq, 0, 0)),
            pl.BlockSpec((1, D_MODEL), const),
            pl.BlockSpec((D_MODEL, FFN_HIDDEN), const),
            pl.BlockSpec((D_MODEL, FFN_HIDDEN), const),
            pl.BlockSpec((FFN_HIDDEN, D_MODEL), const),
        ],
        out_specs=pl.BlockSpec((tm, D_MODEL), lambda i: (i, 0)),
        out_shape=jax.ShapeDtypeStruct(x.shape, F32),
        compiler_params=_cparams(1),
        name="ffn",
    )(x, mod, g.reshape(1, D_MODEL), w1, w3, w2)


def _rope(x, c, sa, sb):
    return x * c + pltpu.roll(x, QK_ROPE // 4, 1) * sa + pltpu.roll(x, LANES - QK_ROPE // 4, 1) * sb


def _keys_values(kv, krope_blk, k_gain, k_ref, v_ref, rope):
    v_ref[...] = kv.astype(BF16)
    nope_lane = lax.broadcasted_iota(jnp.int32, (1, LANES), 1) < QK_NOPE
    for h in range(MLA_HEADS):
        blk = slice(h * LANES, (h + 1) * LANES)
        kh = _rms(jnp.where(nope_lane, kv[:, blk], krope_blk), QK_HEAD) * k_gain
        if rope is not None:
            kh = _rope(kh, *rope)
        k_ref[:, blk] = kh.astype(BF16)


def _gelu_tanh(x):
    return 0.5 * x * (1.0 + jnp.tanh(0.7978845608028654 * (x + 0.044715 * (x * x * x))))


def _mla_pre_kernel(*refs, roped, emit_latents):
    (x_ref, mod_ref, g_ref, w_in_ref, qa_ref, kva_ref, w_qb_ref, w_kvb_ref, qn_ref, kn_ref,
     gv_ref, ws_ref, gb_ref) = refs[:13]
    refs = refs[13:]
    rope = None
    if roped:
        rope = tuple(r[...] for r in refs[:3])
        refs = refs[3:]
    q_ref, k_ref, v_ref, gout_ref = refs[:4]

    hb = _modulated(x_ref[...], g_ref, mod_ref, 1).astype(BF16)
    proj = _dot(hb, w_in_ref[...])
    o_kv, o_kr, o_u = Q_LORA, Q_LORA + KV_LORA, Q_LORA + KV_LORA + LANES
    o_v = o_u + GMLP_WIDTH

    q_lat = _rms(proj[:, :o_kv], Q_LORA) * qa_ref[...]
    q = _dot(q_lat.astype(BF16), w_qb_ref[...])
    q_gain = qn_ref[...] * (QK_HEAD ** -0.5)
    for h in range(MLA_HEADS):
        blk = slice(h * LANES, (h + 1) * LANES)
        qh = _rms(q[:, blk], QK_HEAD) * q_gain
        if roped:
            qh = _rope(qh, *rope)
        q_ref[:, blk] = qh.astype(BF16)

    kv_lat = _rms(proj[:, o_kv:o_kr], KV_LORA) * kva_ref[...]
    krope_blk = proj[:, o_kr:o_u]
    kv = _dot(kv_lat.astype(BF16), w_kvb_ref[...])
    _keys_values(kv, krope_blk, kn_ref[...], k_ref, v_ref, rope)
    if emit_latents:
        refs[4][...] = kv_lat
        refs[5][...] = krope_blk

    u = _gelu_tanh(proj[:, o_u:o_v])
    v = (_rms(_gelu_tanh(proj[:, o_v:]), GMLP_WIDTH) * gv_ref[...]).astype(BF16)
    for n in range(x_ref.shape[0] // GMLP_CHUNK):
        rows = slice(n * GMLP_CHUNK, (n + 1) * GMLP_CHUNK)
        for grp in range(GMLP_GROUPS):
            cols = slice(grp * LANES, (grp + 1) * LANES)
            mixed = _dot(ws_ref[grp], v[rows, cols]) + gb_ref[:, cols]
            gout_ref[rows, cols] = (u[rows, cols] * mixed).astype(BF16)


def _mla_pre(x, mod, g, wts, rope, *, seq, tm, emit_latents):
    t = x.shape[0]
    per_seq = seq // tm
    const = lambda i: (0, 0)
    tok = lambda w: pl.BlockSpec((tm, w), lambda i: (i, 0))
    in_specs = [
        tok(D_MODEL),
        pl.BlockSpec((1, N_MOD, D_MODEL), lambda i: (i // per_seq, 0, 0)),
        pl.BlockSpec((1, D_MODEL), const),
        pl.BlockSpec((D_MODEL, PROJ_W), const),
        pl.BlockSpec((1, Q_LORA), const),
        pl.BlockSpec((1, KV_LORA), const),
        pl.BlockSpec((Q_LORA, HEAD_W), const),
        pl.BlockSpec((KV_LORA, HEAD_W), const),
        pl.BlockSpec((1, LANES), const),
        pl.BlockSpec((1, LANES), const),
        pl.BlockSpec((1, GMLP_WIDTH), const),
        pl.BlockSpec((GMLP_GROUPS, GMLP_CHUNK, GMLP_CHUNK), lambda i: (0, 0, 0)),
        pl.BlockSpec((GMLP_CHUNK, GMLP_WIDTH), const),
    ]
    args = [x, mod, g.reshape(1, D_MODEL), wts["w_in"], wts["q_a_norm"], wts["kv_a_norm"],
            wts["w_qb"], wts["w_kvb"], wts["q_norm"], wts["k_norm"], wts["gmlp_v_norm"],
            wts["gmlp_ws"], wts["gmlp_b"]]
    if rope is not None:
        in_specs += [pl.BlockSpec((tm, LANES), lambda i: (i % per_seq, 0))] * 3
        args += list(rope)
    out_specs = [tok(HEAD_W), tok(HEAD_W), tok(HEAD_W), tok(GMLP_WIDTH)]
    out_shape = [jax.ShapeDtypeStruct((t, HEAD_W), BF16)] * 3 + [
        jax.ShapeDtypeStruct((t, GMLP_WIDTH), BF16)]
    if emit_latents:
        out_specs += [tok(KV_LORA), tok(LANES)]
        out_shape += [jax.ShapeDtypeStruct((t, KV_LORA), F32), jax.ShapeDtypeStruct((t, LANES), F32)]
    return pl.pallas_call(
        functools.partial(_mla_pre_kernel, roped=rope is not None, emit_latents=emit_latents),
        grid=(t // tm,),
        in_specs=in_specs,
        out_specs=out_specs,
        out_shape=out_shape,
        compiler_params=_cparams(1),
        name="mla_pre",
    )(*args)


def _cache_kv_kernel(lat_ref, kr_ref, w_kvb_ref, kn_ref, k_ref, v_ref):
    kv = _dot(lat_ref[...].astype(BF16), w_kvb_ref[...])
    _keys_values(kv, kr_ref[...], kn_ref[...], k_ref, v_ref, None)


def _cache_kv(lat, krope_blk, w_kvb, k_norm, *, tm):
    t = lat.shape[0]
    const = lambda i: (0, 0)
    tok = lambda w: pl.BlockSpec((tm, w), lambda i: (i, 0))
    return pl.pallas_call(
        _cache_kv_kernel,
        grid=(t // tm,),
        in_specs=[tok(KV_LORA), tok(LANES), pl.BlockSpec((KV_LORA, HEAD_W), const),
                  pl.BlockSpec((1, LANES), const)],
        out_specs=[tok(HEAD_W), tok(HEAD_W)],
        out_shape=[jax.ShapeDtypeStruct((t, HEAD_W), BF16)] * 2,
        compiler_params=_cparams(1),
        name="cache_kv",
    )(lat, krope_blk, w_kvb, k_norm)


def _attend(q, kv_blocks):
    m = l = acc = None
    for k_ref, v_ref, rows in kv_blocks:
        s = _dot_nt(q, k_ref[rows, :])
        m_blk = jnp.max(s, axis=-1, keepdims=True)
        if m is None:
            m_new = m_blk
            p = jnp.exp(s - m_new)
            l = jnp.sum(p, axis=-1, keepdims=True)
            acc = _dot(p.astype(BF16), v_ref[rows, :])
        else:
            m_new = jnp.maximum(m, m_blk)
            alpha = jnp.exp(m - m_new)
            p = jnp.exp(s - m_new)
            l = alpha * l + jnp.sum(p, axis=-1, keepdims=True)
            acc = alpha * acc + _dot(p.astype(BF16), v_ref[rows, :])
        m = m_new
    return acc * (1.0 / l)


def _attn_ctx_kernel(q_ref, k_ref, v_ref, o_ref):
    for h in range(MLA_HEADS):
        blk = slice(h * LANES, (h + 1) * LANES)
        head = [(k_ref.at[:, blk], v_ref.at[:, blk], slice(None))]
        o_ref[:, blk] = _attend(q_ref[:, blk], head).astype(BF16)


def _attn_ctx(q, k, v, *, seq):
    t = q.shape[0]
    spec = pl.BlockSpec((seq, HEAD_W), lambda b: (b, 0))
    return pl.pallas_call(
        _attn_ctx_kernel,
        grid=(t // seq,),
        in_specs=[spec, spec, spec],
        out_specs=spec,
        out_shape=jax.ShapeDtypeStruct((t, HEAD_W), BF16),
        compiler_params=_cparams(1),
        name="attn_ctx",
    )(q, k, v)


def _attn_dec_kernel(q_ref, k_ref, v_ref, kc_ref, vc_ref, o_ref, *, tq, ck):
    seq = k_ref.shape[0]
    blocks = [(k_ref, v_ref, slice(c * ck, (c + 1) * ck)) for c in range(seq // ck)]
    blocks.append((kc_ref, vc_ref, slice(None)))

    def body(i, carry):
        rows = pl.ds(pl.multiple_of(i * tq, tq), tq)
        o_ref[rows, :] = _attend(q_ref[rows, :], blocks).astype(BF16)
        return carry

    lax.fori_loop(0, seq // tq, body, 0)


def _attn_dec(q, k, v, kc, vc, *, seq, past, tq, ck):
    t = q.shape[0]
    cur = pl.BlockSpec((seq, LANES), lambda b, h: (b, h))
    old = pl.BlockSpec((past, LANES), lambda b, h: (b, h))
    return pl.pallas_call(
        functools.partial(_attn_dec_kernel, tq=tq, ck=ck),
        grid=(t // seq, MLA_HEADS),
        in_specs=[cur, cur, cur, old, old],
        out_specs=cur,
        out_shape=jax.ShapeDtypeStruct((t, HEAD_W), BF16),
        compiler_params=_cparams(2),
        name="attn_dec",
    )(q, k, v, kc, vc)


def _mix_out_kernel(x_ref, mod_ref, a_ref, g_ref, wa_ref, wg_ref, o_ref):
    mix = _dot(a_ref[...], wa_ref[...]) + _dot(g_ref[...], wg_ref[...])
    o_ref[...] = x_ref[...] + mod_ref[0, 5:6, :] * mix


def _mix_out(x, mod, attn, gout, w_out_a, w_out_g, *, seq, tm):
    t = x.shape[0]
    per_seq = seq // tm
    const = lambda i: (0, 0)
    tok = lambda w: pl.BlockSpec((tm, w), lambda i: (i, 0))
    return pl.pallas_call(
        _mix_out_kernel,
        grid=(t // tm,),
        in_specs=[tok(D_MODEL), pl.BlockSpec((1, N_MOD, D_MODEL), lambda i: (i // per_seq, 0, 0)),
                  tok(HEAD_W), tok(GMLP_WIDTH),
                  pl.BlockSpec((HEAD_W, D_MODEL), const), pl.BlockSpec((GMLP_WIDTH, D_MODEL), const)],
        out_specs=tok(D_MODEL),
        out_shape=jax.ShapeDtypeStruct(x.shape, F32),
        compiler_params=_cparams(1),
        name="mix_out",
    )(x, mod, attn, gout, w_out_a, w_out_g)


def _pool_kernel(x_ref, xp_ref, xn_ref, mod_ref, g_ref, pw_ref, ps_ref, o_ref, buf_a, buf_b,
                 *, per_seq):
    tm = x_ref.shape[0]
    i = pl.program_id(0)
    x = x_ref[...]
    h = _modulated(x, g_ref, mod_ref, 1)
    first = 2 * POOL_HALO
    has_prev = (i % per_seq != 0).astype(F32)
    has_next = (i % per_seq != per_seq - 1).astype(F32)
    pad = jnp.zeros((POOL_HALO, D_MODEL), F32)
    for buf in (buf_a, buf_b):
        buf[0:POOL_HALO, :] = pad
        buf[first + tm + POOL_HALO:, :] = pad
    buf_a[POOL_HALO:first, :] = _modulated(xp_ref[...], g_ref, mod_ref, 1) * has_prev
    buf_a[first:first + tm, :] = h
    buf_a[first + tm:first + tm + POOL_HALO, :] = _modulated(xn_ref[...], g_ref, mod_ref, 1) * has_next

    pos = (i % per_seq) * tm + lax.broadcasted_iota(jnp.int32, (tm, 1), 0)
    seq = per_seq * tm

    span = tm + 2 * POOL_HALO
    src, dst = buf_a, buf_b
    outs = []
    for gi, w in enumerate(POOL_WINDOWS):
        half = w // 2
        back, fwd = (1, 0) if w == 2 else (half // 2, half // 2)
        for cg in range(gi, len(POOL_WINDOWS)):
            cols = slice(cg * POOL_GROUP_CH, (cg + 1) * POOL_GROUP_CH)
            dst[POOL_HALO:POOL_HALO + span, cols] = (
                src[POOL_HALO - back:POOL_HALO - back + span, cols]
                + src[POOL_HALO + fwd:POOL_HALO + fwd + span, cols])
        cols = slice(gi * POOL_GROUP_CH, (gi + 1) * POOL_GROUP_CH)
        count = (jnp.minimum(pos + half, seq) - jnp.maximum(pos - half, 0)).astype(F32)
        mean = dst[first:first + tm, cols] / count
        outs.append(_dot((mean - h[:, cols]).astype(BF16), pw_ref[gi]))
        src, dst = dst, src
    mix = jnp.concatenate(outs, axis=-1) * ps_ref[...]
    o_ref[...] = x + mod_ref[0, 5:6, :] * mix


def _pool(x, mod, g, pool_w, pool_scale, *, seq, tm):
    t = x.shape[0]
    per_seq = seq // tm
    halo_blocks = tm // POOL_HALO
    n_halo = t // POOL_HALO
    const = lambda i: (0, 0)
    return pl.pallas_call(
        functools.partial(_pool_kernel, per_seq=per_seq),
        grid=(t // tm,),
        in_specs=[
            pl.BlockSpec((tm, D_MODEL), lambda i: (i, 0)),
            pl.BlockSpec((POOL_HALO, D_MODEL), lambda i: (jnp.maximum(i * halo_blocks - 1, 0), 0)),
            pl.BlockSpec((POOL_HALO, D_MODEL),
                         lambda i: (jnp.minimum((i + 1) * halo_blocks, n_halo - 1), 0)),
            pl.BlockSpec((1, N_MOD, D_MODEL), lambda i: (i // per_seq, 0, 0)),
            pl.BlockSpec((1, D_MODEL), const),
            pl.BlockSpec((len(POOL_WINDOWS), POOL_GROUP_CH, POOL_GROUP_CH), lambda i: (0, 0, 0)),
            pl.BlockSpec((1, D_MODEL), const),
        ],
        out_specs=pl.BlockSpec((tm, D_MODEL), lambda i: (i, 0)),
        out_shape=jax.ShapeDtypeStruct(x.shape, F32),
        scratch_shapes=[pltpu.VMEM((tm + 4 * POOL_HALO, D_MODEL), F32)] * 2,
        compiler_params=_cparams(1),
        name="pool",
    )(x, x, x, mod, g.reshape(1, D_MODEL), pool_w, pool_scale.reshape(1, D_MODEL))


def _head_blocks(w, width):
    rows = w.shape[0]
    w = w.reshape(rows, MLA_HEADS, width)
    return jnp.pad(w, ((0, 0), (0, 0), (0, LANES - width))).reshape(rows, HEAD_W)


def _mla_weights(j, w_in, q_a_norm, kv_a_norm, w_qb, w_kvb, q_norm, k_norm, gmlp_v_norm, gmlp_ws,
                 gmlp_b, w_out):
    o_kv, o_kr, o_g = Q_LORA, Q_LORA + KV_LORA, Q_LORA + KV_LORA + QK_ROPE
    krope_cols = jnp.pad(w_in[j][:, o_kr:o_g], ((0, 0), (QK_NOPE, LANES - QK_HEAD)))
    w_in_p = jnp.concatenate([w_in[j][:, :o_kr], krope_cols, w_in[j][:, o_g:]], axis=1)
    lane_pad = lambda v: jnp.pad(v, (0, LANES - QK_HEAD)).reshape(1, LANES)
    w_out_a = w_out[j][:MLA_HEADS * V_HEAD].reshape(MLA_HEADS, V_HEAD, D_MODEL)
    w_out_a = jnp.pad(w_out_a, ((0, 0), (LANES - V_HEAD, 0), (0, 0))).reshape(HEAD_W, D_MODEL)
    return {
        "w_in": w_in_p.astype(BF16),
        "q_a_norm": q_a_norm[j].reshape(1, Q_LORA),
        "kv_a_norm": kv_a_norm[j].reshape(1, KV_LORA),
        "w_qb": _head_blocks(w_qb[j], QK_HEAD).astype(BF16),
        "w_kvb": w_kvb[j].astype(BF16),
        "q_norm": lane_pad(q_norm[j]),
        "k_norm": lane_pad(k_norm[j]),
        "gmlp_v_norm": gmlp_v_norm[j].reshape(1, GMLP_WIDTH),
        "gmlp_ws": gmlp_ws[j].astype(BF16),
        "gmlp_b": jnp.repeat(gmlp_b[j].T, LANES, axis=1),
        "w_out_a": w_out_a.astype(BF16),
        "w_out_g": w_out[j][MLA_HEADS * V_HEAD:].astype(BF16),
    }


def _rope_tables(rows):
    row = jnp.repeat(jnp.arange(rows), GRID_W).astype(F32)
    col = jnp.tile(jnp.arange(GRID_W), rows).astype(F32)
    per_axis = QK_ROPE // 2
    inv = ROPE_BASE ** (-jnp.arange(0, per_axis, 2, dtype=F32) / per_axis)
    ang_r, ang_c = row[:, None] * inv, col[:, None] * inv
    n = row.shape[0]
    zeros = jnp.zeros((n, QK_ROPE // 4), F32)
    head = jnp.ones((n, QK_NOPE), F32)
    tail = jnp.zeros((n, LANES - QK_HEAD), F32)
    cr, sr, cc, sc = jnp.cos(ang_r), jnp.sin(ang_r), jnp.cos(ang_c), jnp.sin(ang_c)
    c = jnp.concatenate([head, cr, cr, cc, cc, tail], axis=1)
    sa = jnp.concatenate([0 * head, zeros, sr, zeros, sc, tail], axis=1)
    sb = jnp.concatenate([0 * head, -sr, zeros, -sc, zeros, tail], axis=1)
    return c, sa, sb


def _trunk(x, mods, norm_g, ffn_w, mla_w, pool_w, pool_scale, cache, rope, *, seq, tm):
    latents = []
    for i in range(DEPTH):
        w1, w3, w2 = ffn_w
        x = _ffn(x, mods[i], norm_g[i, 0], w1[i, 0], w3[i, 0], w2[i, 0], k=0, seq=seq, tm=tm)
        j = i // 2
        if i % 2 == 0:
            wts = mla_w[j]
            outs = _mla_pre(x, mods[i], norm_g[i, 1], wts, rope, seq=seq, tm=tm,
                            emit_latents=cache is None)
            q, k, v, gout = outs[:4]
            if cache is None:
                latents.append(outs[4:])
                attn = _attn_ctx(q, k, v, seq=seq)
            else:
                lat, krope_blk, past = cache
                kc, vc = _cache_kv(lat[j], krope_blk[j], wts["w_kvb"], wts["k_norm"], tm=tm)
                attn = _attn_dec(q, k, v, kc, vc, seq=seq, past=past, tq=256, ck=512)
            x = _mix_out(x, mods[i], attn, gout, wts["w_out_a"], wts["w_out_g"], seq=seq, tm=tm)
        else:
            x = _pool(x, mods[i], norm_g[i, 1], pool_w[j], pool_scale[j], seq=seq, tm=tm)
        x = _ffn(x, mods[i], norm_g[i, 2], w1[i, 1], w3[i, 1], w2[i, 1], k=2, seq=seq, tm=tm)
    return x, latents


def kernel(x_prompt, x_sample, cache_ckv, cache_krope, c, c_ctx, w_mod, b_mod, norm_g, ffn_w1, ffn_w3,
           ffn_w2, w_in, q_a_norm, kv_a_norm, w_qb, w_kvb, q_norm, k_norm, gmlp_v_norm, gmlp_ws, gmlp_b,
           w_out, pool_w, pool_scale):
    batch, seq, _ = x_prompt.shape
    dec_batch, dec_seq, _ = x_sample.shape
    past = cache_ckv.shape[2]
    n_mla = w_in.shape[0]

    rows = -(-(1 + dec_batch) // SUBLANES) * SUBLANES
    cvecs = jnp.zeros((rows, D_MODEL), F32).at[0].set(c_ctx).at[1:1 + dec_batch].set(c)
    m_all = _modulation(cvecs, w_mod, b_mod).reshape(DEPTH, rows, N_MOD, D_MODEL)
    mods_ctx = jnp.broadcast_to(m_all[:, 0:1], (DEPTH, batch, N_MOD, D_MODEL))
    mods_dec = m_all[:, 1:1 + dec_batch]

    ffn_w = (ffn_w1.astype(BF16), ffn_w3.astype(BF16), ffn_w2.astype(BF16))
    mla_w = [_mla_weights(j, w_in, q_a_norm, kv_a_norm, w_qb, w_kvb, q_norm, k_norm, gmlp_v_norm,
                          gmlp_ws, gmlp_b, w_out) for j in range(n_mla)]
    pool_wb = pool_w.astype(BF16)

    y_prompt, latents = _trunk(x_prompt.reshape(batch * seq, D_MODEL), mods_ctx, norm_g, ffn_w, mla_w,
                               pool_wb, pool_scale, None, None, seq=seq, tm=seq)
    new_ckv = jnp.stack([l[0].reshape(batch, seq, KV_LORA) for l in latents], axis=1)
    new_krope = jnp.stack(
        [l[1][:, QK_NOPE:QK_HEAD].reshape(batch, seq, QK_ROPE) for l in latents], axis=1)

    cache_lat = cache_ckv.transpose(1, 0, 2, 3).reshape(n_mla, dec_batch * past, KV_LORA)
    cache_kr = cache_krope.transpose(1, 0, 2, 3).reshape(n_mla, dec_batch * past, QK_ROPE)
    cache_kr = jnp.pad(cache_kr, ((0, 0), (0, 0), (QK_NOPE, LANES - QK_HEAD)))
    rope = _rope_tables(dec_seq // GRID_W)
    y_sample, _ = _trunk(x_sample.reshape(dec_batch * dec_seq, D_MODEL), mods_dec, norm_g, ffn_w, mla_w,
                         pool_wb, pool_scale, (cache_lat, cache_kr, past), rope, seq=dec_seq, tm=512)
    return (y_prompt.reshape(batch, seq, D_MODEL), y_sample.reshape(dec_batch, dec_seq, D_MODEL),
            new_ckv, new_krope)
```

```python
import functools

import jax
import jax.numpy as jnp
from jax import lax
from jax.experimental import pallas as pl
from jax.experimental.pallas import tpu as pltpu

D_MODEL = 1024
DEPTH = 4
N_MOD = 9
FFN_HIDDEN = 2816
EPS = 1e-6
MLA_HEADS = 8
Q_LORA = 256
KV_LORA = 128
QK_NOPE = 64
QK_ROPE = 32
V_HEAD = 64
QK_HEAD = QK_NOPE + QK_ROPE
GRID_W = 64
ROPE_BASE = 10000.0
LOG2_E = 1.4426950408889634
GMLP_GROUPS = 4
GMLP_CHUNK = 128
GMLP_WIDTH = 512
POOL_WINDOWS = (2, 4, 8, 16)
POOL_GROUP_CH = D_MODEL // len(POOL_WINDOWS)
POOL_HALO = max(POOL_WINDOWS) // 2

LANES = 128
SUBLANES = 8
MXU_TILE = 256
HEAD_W = MLA_HEADS * LANES
PROJ_W = Q_LORA + KV_LORA + 2 * LANES + 2 * GMLP_WIDTH
VMEM_LIMIT = 56 * 1024 * 1024

TOKEN_TILE = 512
FFN_TILE = 1024
ATTN_Q_TILE = 256
ATTN_KEY_CHUNK = 512
ATTN_TILES_IN_FLIGHT = 3

BF16 = jnp.bfloat16
F32 = jnp.float32


def _cparams(n_axes):
    return pltpu.CompilerParams(
        dimension_semantics=("arbitrary",) * n_axes, vmem_limit_bytes=VMEM_LIMIT)


def _rms(x, width):
    ss = jnp.sum(x * x, axis=-1, keepdims=True) * (1.0 / width)
    return x * lax.rsqrt(ss + EPS)


def _modulated(x, g_ref, mod_ref, k):
    shift = mod_ref[0, 3 * k:3 * k + 1, :]
    scale = mod_ref[0, 3 * k + 1:3 * k + 2, :]
    return (_rms(x, D_MODEL) * g_ref[...]) * (1.0 + scale) + shift


def _mod_spec(cond, tm):
    layer, row0, mod_seq = cond
    tiles_per_row = mod_seq // tm
    return pl.BlockSpec((None, 1, N_MOD, D_MODEL), lambda i: (layer, row0 + i // tiles_per_row, 0, 0))


def _gain_spec(layer, k):
    return pl.BlockSpec((None, 1, D_MODEL), lambda i: (3 * layer + k, 0, 0))


def _resident(shape, index):
    return pl.BlockSpec(shape, lambda i: index, pipeline_mode=pl.Buffered(1))


def _dot(a, b):
    return jnp.dot(a, b, preferred_element_type=F32)


def _dot_nt(a, b):
    return lax.dot_general(a, b, (((1,), (1,)), ((), ())), preferred_element_type=F32)


def _mod_kernel(c_ref, w_ref, b_ref, o_ref):
    c = c_ref[...]
    a = (c * (1.0 / (1.0 + jnp.exp(-c)))).astype(BF16)
    o_ref[0] = _dot(a, w_ref[0].astype(BF16)) + b_ref[0]


def _modulation(cvecs, w_mod, b_mod):
    rows = cvecs.shape[0]
    tn = D_MODEL
    n_out = N_MOD * D_MODEL
    return pl.pallas_call(
        _mod_kernel,
        grid=(DEPTH, n_out // tn),
        in_specs=[
            pl.BlockSpec((rows, D_MODEL), lambda i, j: (0, 0)),
            pl.BlockSpec((1, D_MODEL, tn), lambda i, j: (i, 0, j)),
            pl.BlockSpec((1, 1, tn), lambda i, j: (i, 0, j)),
        ],
        out_specs=pl.BlockSpec((1, rows, tn), lambda i, j: (i, 0, j)),
        out_shape=jax.ShapeDtypeStruct((DEPTH, rows, n_out), F32),
        compiler_params=_cparams(2),
        name="modulation",
    )(cvecs, w_mod, b_mod.reshape(DEPTH, 1, n_out))


def _pool_mix(h, prev_h, next_h, pw_ref, ps_ref, buf_a, buf_b, *, seq, pos0):
    rows = h.shape[0]
    seg = min(rows, seq)
    whole = seg == seq
    first = 2 * POOL_HALO
    span = seg + 2 * POOL_HALO
    pad = jnp.zeros((POOL_HALO, D_MODEL), F32)
    for buf in (buf_a, buf_b):
        buf[0:POOL_HALO, :] = pad
        buf[first + seg + POOL_HALO:, :] = pad
    pos = (0 if whole else pos0) + lax.broadcasted_iota(jnp.int32, (seg, 1), 0)
    inv_counts = [1.0 / (jnp.minimum(pos + w // 2, seq) - jnp.maximum(pos - w // 2, 0)).astype(F32)
                  for w in POOL_WINDOWS]
    mixes = []
    for s in range(rows // seg):
        hs = h[s * seg:(s + 1) * seg, :]
        buf_a[POOL_HALO:first, :] = pad if whole else prev_h
        buf_a[first:first + seg, :] = hs
        buf_a[first + seg:first + seg + POOL_HALO, :] = pad if whole else next_h
        src, dst = buf_a, buf_b
        outs = []
        for gi, w in enumerate(POOL_WINDOWS):
            half = w // 2
            back, fwd = (1, 0) if w == 2 else (half // 2, half // 2)
            for cg in range(gi, len(POOL_WINDOWS)):
                cols = slice(cg * POOL_GROUP_CH, (cg + 1) * POOL_GROUP_CH)
                dst[POOL_HALO:POOL_HALO + span, cols] = (
                    src[POOL_HALO - back:POOL_HALO - back + span, cols]
                    + src[POOL_HALO + fwd:POOL_HALO + fwd + span, cols])
            cols = slice(gi * POOL_GROUP_CH, (gi + 1) * POOL_GROUP_CH)
            mean = dst[first:first + seg, cols] * inv_counts[gi]
            outs.append(_dot((mean - hs[:, cols]).astype(BF16), pw_ref[gi]))
            src, dst = dst, src
        mixes.append(jnp.concatenate(outs, axis=-1))
    mix = mixes[0] if len(mixes) == 1 else jnp.concatenate(mixes, axis=0)
    return mix * ps_ref[...]


def _ffn_kernel(*refs, k, pre, seq):
    refs = list(refs)
    x_ref = refs.pop(0)
    if pre == "mix":
        attn_ref, gout_ref, wa_ref, wg_ref = refs[:4]
        refs = refs[4:]
    elif pre == "pool":
        xp_ref, xn_ref, gp_ref, pw_ref, ps_ref = refs[:5]
        refs = refs[5:]
    mod_ref, g_ref, w1_ref, w3_ref, w2_ref, o_ref, gated_ref = refs[:7]

    x = x_ref[...]
    if pre == "mix":
        mix = _dot(attn_ref[...], wa_ref[...]) + _dot(gout_ref[...], wg_ref[...])
        x = x + mod_ref[0, 5:6, :] * mix
    elif pre == "pool":
        tm = x.shape[0]
        tiles = max(seq // tm, 1)
        j = pl.program_id(0) % tiles
        h = _modulated(x, gp_ref, mod_ref, 1)
        prev_h = _modulated(xp_ref[...], gp_ref, mod_ref, 1) * (j != 0).astype(F32)
        next_h = _modulated(xn_ref[...], gp_ref, mod_ref, 1) * (j != tiles - 1).astype(F32)
        mix = _pool_mix(h, prev_h, next_h, pw_ref, ps_ref, refs[7], refs[8], seq=seq, pos0=j * tm)
        x = x + mod_ref[0, 5:6, :] * mix

    hb = _modulated(x, g_ref, mod_ref, k).astype(BF16)
    for c in range(FFN_HIDDEN // MXU_TILE):
        cols = slice(c * MXU_TILE, (c + 1) * MXU_TILE)
        a = _dot(hb, w1_ref[:, cols])
        b = _dot(hb, w3_ref[:, cols])
        gated_ref[:, cols] = (a * (1.0 / (1.0 + jnp.exp(-a))) * b).astype(BF16)
    gate = mod_ref[0, 3 * k + 2:3 * k + 3, :]
    o_ref[...] = x + (0.5 * gate) * _dot(gated_ref[...], w2_ref[...])


def _ffn(x, mods, gains, ffn_w, *, cond, k, tm, pre=None, pre_args=(), seq=None):
    t = x.shape[0]
    layer = cond[0]
    which = 0 if k == 0 else 1
    tok = lambda w: pl.BlockSpec((tm, w), lambda i: (i, 0))
    in_specs, args, scratch = [tok(D_MODEL)], [x], [pltpu.VMEM((tm, FFN_HIDDEN), BF16)]
    if pre == "mix":
        attn, gout, w_out_a, w_out_g = pre_args
        in_specs += [tok(HEAD_W), tok(GMLP_WIDTH), _resident((HEAD_W, D_MODEL), (0, 0)),
                     _resident((GMLP_WIDTH, D_MODEL), (0, 0))]
        args += [attn, gout, w_out_a, w_out_g]
    elif pre == "pool":
        pool_w, pool_scale, j = pre_args
        halo_blocks, n_halo = tm // POOL_HALO, t // POOL_HALO
        in_specs += [
            pl.BlockSpec((POOL_HALO, D_MODEL), lambda i: (jnp.maximum(i * halo_blocks - 1, 0), 0)),
            pl.BlockSpec((POOL_HALO, D_MODEL), lambda i: (jnp.minimum((i + 1) * halo_blocks, n_halo - 1), 0)),
            _gain_spec(layer, 1),
            _resident((None, len(POOL_WINDOWS), POOL_GROUP_CH, POOL_GROUP_CH), (j, 0, 0, 0)),
            _resident((None, 1, D_MODEL), (j, 0, 0)),
        ]
        args += [x, x, gains, pool_w, pool_scale]
        scratch += [pltpu.VMEM((min(tm, seq) + 4 * POOL_HALO, D_MODEL), F32)] * 2
    w1, w3, w2 = ffn_w
    in_specs += [
        _mod_spec(cond, tm),
        _gain_spec(layer, k),
        _resident((None, None, D_MODEL, FFN_HIDDEN), (layer, which, 0, 0)),
        _resident((None, None, D_MODEL, FFN_HIDDEN), (layer, which, 0, 0)),
        _resident((None, None, FFN_HIDDEN, D_MODEL), (layer, which, 0, 0)),
    ]
    args += [mods, gains, w1, w3, w2]
    return pl.pallas_call(
        functools.partial(_ffn_kernel, k=k, pre=pre, seq=seq),
        grid=(t // tm,),
        in_specs=in_specs,
        out_specs=tok(D_MODEL),
        out_shape=jax.ShapeDtypeStruct(x.shape, F32),
        scratch_shapes=scratch,
        compiler_params=_cparams(1),
        name="ffn" if pre is None else pre + "_ffn",
    )(*args)


def _head_gains(gain, gain_partner, rope):
    if rope is None:
        return gain, None
    cos, sin = rope
    return gain * cos, gain_partner * sin


def _norm_rope(x, partner, gains):
    ss = jnp.sum(x * x, axis=-1, keepdims=True) * (1.0 / QK_HEAD)
    y = x * gains[0]
    if gains[1] is not None:
        y = y + partner * gains[1]
    return y * lax.rsqrt(ss + EPS)


def _keys_values(kv_lat, krope_blk, krope_partner, w_kvb_ref, w_kvbt_ref, gains, k_ref, vt_ref):
    vt_ref[...] = _dot_nt(w_kvbt_ref[...], kv_lat).astype(BF16)
    kv = _dot(kv_lat, w_kvb_ref[...])
    nope_lane = lax.broadcasted_iota(jnp.int32, (1, LANES), 1) < QK_NOPE
    for h in range(MLA_HEADS):
        blk = slice(h * LANES, (h + 1) * LANES)
        kh = jnp.where(nope_lane, kv[:, blk], krope_blk)
        k_ref[:, blk] = _norm_rope(kh, krope_partner, gains).astype(BF16)


def _gelu_tanh(x):
    return 0.5 * x * (1.0 + jnp.tanh(0.7978845608028654 * (x + 0.044715 * (x * x * x))))


def _mla_pre_kernel(*refs, roped, emit_latents):
    (x_ref, mod_ref, g_ref, w_in_ref, qa_ref, kva_ref, w_qb_ref, w_kvb_ref, w_kvbt_ref, qn_ref, kn_ref,
     gv_ref, ws_ref, gb_ref) = refs[:14]
    refs = refs[14:]
    rope = None
    if roped:
        w_qbp_ref = refs[0]
        rope = (refs[1][...], refs[2][...])
        refs = refs[3:]
    q_ref, k_ref, vt_ref, gout_ref = refs[:4]

    hb = _modulated(x_ref[...], g_ref, mod_ref, 1).astype(BF16)
    proj = _dot(hb, w_in_ref[...])
    o_kv, o_kr = Q_LORA, Q_LORA + KV_LORA
    o_kp, o_u = o_kr + LANES, o_kr + 2 * LANES
    o_v = o_u + GMLP_WIDTH

    q_lat = (_rms(proj[:, :o_kv], Q_LORA) * qa_ref[...]).astype(BF16)
    q = _dot(q_lat, w_qb_ref[...])
    q_partner = _dot(q_lat, w_qbp_ref[...]) if roped else None
    q_gains = _head_gains(qn_ref[0:1, :] * (QK_HEAD ** -0.5 * LOG2_E),
                          qn_ref[1:2, :] * (QK_HEAD ** -0.5 * LOG2_E), rope)
    for h in range(MLA_HEADS):
        blk = slice(h * LANES, (h + 1) * LANES)
        qp = q_partner[:, blk] if roped else None
        q_ref[:, blk] = _norm_rope(q[:, blk], qp, q_gains).astype(BF16)

    kv_lat = _rms(proj[:, o_kv:o_kr], KV_LORA) * kva_ref[...]
    krope_blk = proj[:, o_kr:o_kp]
    k_gains = _head_gains(kn_ref[0:1, :], kn_ref[1:2, :], rope)
    _keys_values(kv_lat.astype(BF16), krope_blk, proj[:, o_kp:o_u], w_kvb_ref, w_kvbt_ref, k_gains,
                 k_ref, vt_ref)
    if emit_latents:
        refs[4][...] = kv_lat
        refs[5][...] = krope_blk

    u = _gelu_tanh(proj[:, o_u:o_v])
    v = (_rms(_gelu_tanh(proj[:, o_v:]), GMLP_WIDTH) * gv_ref[...]).astype(BF16)
    for n in range(x_ref.shape[0] // GMLP_CHUNK):
        rows = slice(n * GMLP_CHUNK, (n + 1) * GMLP_CHUNK)
        for grp in range(GMLP_GROUPS):
            cols = slice(grp * LANES, (grp + 1) * LANES)
            mixed = _dot(ws_ref[grp], v[rows, cols]) + gb_ref[:, cols]
            gout_ref[rows, cols] = (u[rows, cols] * mixed).astype(BF16)


def _mla_pre(x, mods, gains, wts, rope, *, cond, seq, tm, emit_latents):
    t = x.shape[0]
    per_seq = seq // tm
    const = lambda i: (0, 0)
    tok = lambda w: pl.BlockSpec((tm, w), lambda i: (i, 0))
    in_specs = [
        tok(D_MODEL),
        _mod_spec(cond, tm),
        _gain_spec(cond[0], 1),
        pl.BlockSpec((D_MODEL, PROJ_W), const),
        pl.BlockSpec((1, Q_LORA), const),
        pl.BlockSpec((1, KV_LORA), const),
        pl.BlockSpec((Q_LORA, HEAD_W), const),
        pl.BlockSpec((KV_LORA, HEAD_W), const),
        pl.BlockSpec((HEAD_W, KV_LORA), const),
        pl.BlockSpec((2, LANES), const),
        pl.BlockSpec((2, LANES), const),
        pl.BlockSpec((1, GMLP_WIDTH), const),
        pl.BlockSpec((GMLP_GROUPS, GMLP_CHUNK, GMLP_CHUNK), lambda i: (0, 0, 0)),
        pl.BlockSpec((GMLP_CHUNK, GMLP_WIDTH), const),
    ]
    args = [x, mods, gains, wts["w_in"], wts["q_a_norm"], wts["kv_a_norm"],
            wts["w_qb"], wts["w_kvb"], wts["w_kvbt"], wts["q_norm"], wts["k_norm"], wts["gmlp_v_norm"],
            wts["gmlp_ws"], wts["gmlp_b"]]
    if rope is not None:
        in_specs += [pl.BlockSpec((Q_LORA, HEAD_W), const)]
        in_specs += [pl.BlockSpec((tm, LANES), lambda i: (i % per_seq, 0))] * 2
        args += [wts["w_qb_partner"]] + list(rope)
    out_specs = [tok(HEAD_W), tok(HEAD_W), pl.BlockSpec((HEAD_W, tm), lambda i: (0, i)), tok(GMLP_WIDTH)]
    out_shape = [jax.ShapeDtypeStruct((t, HEAD_W), BF16)] * 2 + [
        jax.ShapeDtypeStruct((HEAD_W, t), BF16), jax.ShapeDtypeStruct((t, GMLP_WIDTH), BF16)]
    if emit_latents:
        out_specs += [tok(KV_LORA), tok(LANES)]
        out_shape += [jax.ShapeDtypeStruct((t, KV_LORA), F32), jax.ShapeDtypeStruct((t, LANES), F32)]
    return pl.pallas_call(
        functools.partial(_mla_pre_kernel, roped=rope is not None, emit_latents=emit_latents),
        grid=(t // tm,),
        in_specs=in_specs,
        out_specs=out_specs,
        out_shape=out_shape,
        compiler_params=_cparams(1),
        name="mla_pre",
    )(*args)


def _cache_kv_kernel(lat_ref, kr_ref, w_kvb_ref, w_kvbt_ref, kn_ref, k_ref, vt_ref):
    gains = _head_gains(kn_ref[0:1, :], None, None)
    _keys_values(lat_ref[...].astype(BF16), kr_ref[...], None, w_kvb_ref, w_kvbt_ref, gains, k_ref, vt_ref)


def _cache_kv(lat, krope_blk, w_kvb, w_kvbt, k_norm, *, tm):
    t = lat.shape[0]
    const = lambda i: (0, 0)
    tok = lambda w: pl.BlockSpec((tm, w), lambda i: (i, 0))
    return pl.pallas_call(
        _cache_kv_kernel,
        grid=(t // tm,),
        in_specs=[tok(KV_LORA), tok(LANES), pl.BlockSpec((KV_LORA, HEAD_W), const),
                  pl.BlockSpec((HEAD_W, KV_LORA), const), pl.BlockSpec((2, LANES), const)],
        out_specs=[tok(HEAD_W), pl.BlockSpec((HEAD_W, tm), lambda i: (0, i))],
        out_shape=[jax.ShapeDtypeStruct((t, HEAD_W), BF16), jax.ShapeDtypeStruct((HEAD_W, t), BF16)],
        compiler_params=_cparams(1),
        name="cache_kv",
    )(lat, krope_blk, w_kvb, w_kvbt, k_norm)


def _sublane_groups(x):
    return x.reshape(x.shape[0] // SUBLANES, SUBLANES, x.shape[1])


def _attn_ctx_kernel(q_ref, k_ref, vt_ref, o_ref):
    for h in range(MLA_HEADS):
        blk = slice(h * LANES, (h + 1) * LANES)
        st = _dot_nt(k_ref[:, blk], q_ref[:, blk])
        pt = jnp.exp2(st - jnp.max(st, axis=0, keepdims=True))
        inv = 1.0 / jnp.sum(pt, axis=0, keepdims=True)
        ot = _dot(vt_ref[blk, :], pt.astype(BF16)) * inv
        o_ref[:, blk] = ot.T.astype(BF16)


def _attn_ctx(q, k, vt, *, seq):
    t = q.shape[0]
    spec = pl.BlockSpec((seq, HEAD_W), lambda b: (b, 0))
    return pl.pallas_call(
        _attn_ctx_kernel,
        grid=(t // seq,),
        in_specs=[spec, spec, pl.BlockSpec((HEAD_W, seq), lambda b: (0, b))],
        out_specs=spec,
        out_shape=jax.ShapeDtypeStruct((t, HEAD_W), BF16),
        compiler_params=_cparams(1),
        name="attn_ctx",
    )(q, k, vt)


def _attn_dec_kernel(q_ref, k_ref, vt_ref, kc_ref, vct_ref, o_ref, *scratch, tq, ck):
    seq, past = k_ref.shape[0], kc_ref.shape[0]
    n = seq // tq
    depth = len(scratch) // 2
    lag = depth - 1
    bufs = list(zip(scratch[:depth], scratch[depth:]))
    key_blocks = [(k_ref, vt_ref, c * ck, c * ck, ck) for c in range(seq // ck)]
    key_blocks.append((kc_ref, vct_ref, 0, seq, past))

    def tile_rows(i):
        return pl.ds(pl.multiple_of(i * tq, tq), tq)

    def step(nxt, cur):
        if nxt is not None:
            q = q_ref[tile_rows(nxt[0]), :]
            mx = None
        if cur is not None:
            m = cur[2][...]
            l = jnp.zeros((SUBLANES, tq), F32)
            acc = jnp.zeros((LANES, tq), F32)
        for keys, values_t, r0, s0, nr in key_blocks:
            if nxt is not None:
                st = _dot_nt(keys[r0:r0 + nr, :], q)
                nxt[1][s0:s0 + nr, :] = st
                blk_max = jnp.max(_sublane_groups(st), axis=0)
                mx = blk_max if mx is None else jnp.maximum(mx, blk_max)
            if cur is not None:
                p = jnp.exp2(_sublane_groups(cur[1][s0:s0 + nr, :]) - m[None])
                l = l + jnp.sum(p, axis=0)
                acc = acc + _dot(values_t[:, r0:r0 + nr], p.reshape(nr, tq).astype(BF16))
        if nxt is not None:
            nxt[2][...] = jnp.broadcast_to(jnp.max(mx, axis=0, keepdims=True), (SUBLANES, tq))
        if cur is not None:
            inv = 1.0 / jnp.sum(l, axis=0, keepdims=True)
            o_ref[tile_rows(cur[0]), :] = (acc * inv).T.astype(BF16)

    tile = lambda i, slot: (i,) + bufs[slot % depth]
    for i in range(lag):
        step(tile(i, i), None)

    def body(j, carry):
        for r in range(depth):
            step(tile(depth * j + r + lag, r + lag), tile(depth * j + r, r))
        return carry

    full = (n - lag) // depth
    lax.fori_loop(0, full, body, 0)
    for i in range(full * depth, n - lag):
        step(tile(i + lag, i + lag), tile(i, i))
    for i in range(n - lag, n):
        step(None, tile(i, i))


def _attn_dec(q, k, vt, kc, vct, *, seq, past, tq, ck):
    t = q.shape[0]
    assert seq % tq == 0 and seq // tq >= ATTN_TILES_IN_FLIGHT and seq % ck == 0
    cur = pl.BlockSpec((seq, LANES), lambda b, h: (b, h))
    cur_t = pl.BlockSpec((LANES, seq), lambda b, h: (h, b))
    old = pl.BlockSpec((past, LANES), lambda b, h: (b, h))
    old_t = pl.BlockSpec((LANES, past), lambda b, h: (h, b))
    score_buf = pltpu.VMEM((seq + past, tq), F32)
    max_buf = pltpu.VMEM((SUBLANES, tq), F32)
    return pl.pallas_call(
        functools.partial(_attn_dec_kernel, tq=tq, ck=ck),
        grid=(t // seq, MLA_HEADS),
        in_specs=[cur, cur, cur_t, old, old_t],
        out_specs=cur,
        out_shape=jax.ShapeDtypeStruct((t, HEAD_W), BF16),
        scratch_shapes=[score_buf] * ATTN_TILES_IN_FLIGHT + [max_buf] * ATTN_TILES_IN_FLIGHT,
        compiler_params=_cparams(2),
        name="attn_dec",
    )(q, k, vt, kc, vct)


def _head_blocks(w, width):
    rows = w.shape[0]
    w = w.reshape(rows, MLA_HEADS, width)
    return jnp.pad(w, ((0, 0), (0, 0), (0, LANES - width))).reshape(rows, HEAD_W)


def _rope_partner(v):
    quarter = QK_ROPE // 4
    return v.reshape(v.shape[:-1] + (2, 2, quarter))[..., ::-1, :].reshape(v.shape)


def _rope_lanes(v, fill=0.0):
    pad = [(0, 0)] * (v.ndim - 1) + [(QK_NOPE, LANES - QK_HEAD)]
    return jnp.pad(v, pad, constant_values=fill)


def _mla_weights(j, w_in, q_a_norm, kv_a_norm, w_qb, w_kvb, q_norm, k_norm, gmlp_v_norm, gmlp_ws,
                 gmlp_b, w_out):
    o_kv, o_kr, o_g = Q_LORA, Q_LORA + KV_LORA, Q_LORA + KV_LORA + QK_ROPE
    w_krope = w_in[j][:, o_kr:o_g]
    w_in_p = jnp.concatenate([w_in[j][:, :o_kr], _rope_lanes(w_krope), _rope_lanes(_rope_partner(w_krope)),
                              w_in[j][:, o_g:]], axis=1)
    gain_rows = lambda g: jnp.stack([jnp.pad(g, (0, LANES - QK_HEAD)), _rope_lanes(_rope_partner(g[QK_NOPE:]))])
    w_q_rope = w_qb[j].reshape(Q_LORA, MLA_HEADS, QK_HEAD)[:, :, QK_NOPE:]
    w_out_a = w_out[j][:MLA_HEADS * V_HEAD].reshape(MLA_HEADS, V_HEAD, D_MODEL)
    w_out_a = jnp.pad(w_out_a, ((0, 0), (LANES - V_HEAD, 0), (0, 0))).reshape(HEAD_W, D_MODEL)
    return {
        "w_in": w_in_p.astype(BF16),
        "q_a_norm": q_a_norm[j].reshape(1, Q_LORA),
        "kv_a_norm": kv_a_norm[j].reshape(1, KV_LORA),
        "w_qb": _head_blocks(w_qb[j], QK_HEAD).astype(BF16),
        "w_qb_partner": _rope_lanes(_rope_partner(w_q_rope)).reshape(Q_LORA, HEAD_W).astype(BF16),
        "w_kvb": w_kvb[j].astype(BF16),
        "w_kvbt": w_kvb[j].T.astype(BF16),
        "q_norm": gain_rows(q_norm[j]),
        "k_norm": gain_rows(k_norm[j]),
        "gmlp_v_norm": gmlp_v_norm[j].reshape(1, GMLP_WIDTH),
        "gmlp_ws": gmlp_ws[j].astype(BF16),
        "gmlp_b": jnp.repeat(gmlp_b[j].T, LANES, axis=1),
        "w_out_a": w_out_a.astype(BF16),
        "w_out_g": w_out[j][MLA_HEADS * V_HEAD:].astype(BF16),
    }


def _rope_tables(rows):
    row = jnp.repeat(jnp.arange(rows), GRID_W).astype(F32)
    col = jnp.tile(jnp.arange(GRID_W), rows).astype(F32)
    per_axis = QK_ROPE // 2
    inv = ROPE_BASE ** (-jnp.arange(0, per_axis, 2, dtype=F32) / per_axis)
    ang_r, ang_c = row[:, None] * inv, col[:, None] * inv
    cr, sr, cc, sc = jnp.cos(ang_r), jnp.sin(ang_r), jnp.cos(ang_c), jnp.sin(ang_c)
    cos = _rope_lanes(jnp.concatenate([cr, cr, cc, cc], axis=1), fill=1.0)
    sin = _rope_lanes(jnp.concatenate([-sr, sr, -sc, sc], axis=1))
    return cos, sin


def _trunk(x, mods, gains, ffn_w, mla_w, pool_w, pool_scale, cache, rope, *, seq, mod_row0, mod_seq):
    tm = min(TOKEN_TILE, mod_seq)
    ffn_tm = min(FFN_TILE, mod_seq)
    assert ffn_tm % min(ffn_tm, seq) == 0 and max(ffn_tm, seq) % min(ffn_tm, seq) == 0
    latents = []
    for i in range(DEPTH):
        cond = (i, mod_row0, mod_seq)
        x = _ffn(x, mods, gains, ffn_w, cond=cond, k=0, tm=ffn_tm)
        j = i // 2
        if i % 2 == 0:
            wts = mla_w[j]
            outs = _mla_pre(x, mods, gains, wts, rope, cond=cond, seq=seq, tm=tm, emit_latents=cache is None)
            q, k, vt, gout = outs[:4]
            if cache is None:
                latents.append(outs[4:])
                attn = _attn_ctx(q, k, vt, seq=seq)
            else:
                lat, krope_blk, past = cache
                kc, vct = _cache_kv(lat[j], krope_blk[j], wts["w_kvb"], wts["w_kvbt"], wts["k_norm"], tm=tm)
                attn = _attn_dec(q, k, vt, kc, vct, seq=seq, past=past, tq=ATTN_Q_TILE, ck=ATTN_KEY_CHUNK)
            pre, pre_args = "mix", (attn, gout, wts["w_out_a"], wts["w_out_g"])
        else:
            pre, pre_args = "pool", (pool_w, pool_scale, j)
        x = _ffn(x, mods, gains, ffn_w, cond=cond, k=2, tm=ffn_tm, pre=pre, pre_args=pre_args, seq=seq)
    return x, latents


def kernel(x_prompt, x_sample, cache_ckv, cache_krope, c, c_ctx, w_mod, b_mod, norm_g, ffn_w1, ffn_w3,
           ffn_w2, w_in, q_a_norm, kv_a_norm, w_qb, w_kvb, q_norm, k_norm, gmlp_v_norm, gmlp_ws, gmlp_b,
           w_out, pool_w, pool_scale):
    batch, seq, _ = x_prompt.shape
    dec_batch, dec_seq, _ = x_sample.shape
    past = cache_ckv.shape[2]
    n_mla = w_in.shape[0]

    rows = -(-(1 + dec_batch) // SUBLANES) * SUBLANES
    cvecs = jnp.zeros((rows, D_MODEL), F32).at[0].set(c_ctx).at[1:1 + dec_batch].set(c)
    mods = _modulation(cvecs, w_mod, b_mod).reshape(DEPTH, rows, N_MOD, D_MODEL)

    gains = norm_g.reshape(DEPTH * 3, 1, D_MODEL)
    ffn_w = (ffn_w1.astype(BF16), ffn_w3.astype(BF16), ffn_w2.astype(BF16))
    mla_w = [_mla_weights(j, w_in, q_a_norm, kv_a_norm, w_qb, w_kvb, q_norm, k_norm, gmlp_v_norm,
                          gmlp_ws, gmlp_b, w_out) for j in range(n_mla)]
    pool_wb = pool_w.astype(BF16)
    pool_sc = pool_scale.reshape(-1, 1, D_MODEL)

    y_prompt, latents = _trunk(x_prompt.reshape(batch * seq, D_MODEL), mods, gains, ffn_w, mla_w,
                               pool_wb, pool_sc, None, None, seq=seq, mod_row0=0, mod_seq=batch * seq)
    new_ckv = jnp.stack([l[0].reshape(batch, seq, KV_LORA) for l in latents], axis=1)
    new_krope = jnp.stack(
        [l[1][:, QK_NOPE:QK_HEAD].reshape(batch, seq, QK_ROPE) for l in latents], axis=1)

    cache_lat = cache_ckv.transpose(1, 0, 2, 3).reshape(n_mla, dec_batch * past, KV_LORA)
    cache_kr = cache_krope.transpose(1, 0, 2, 3).reshape(n_mla, dec_batch * past, QK_ROPE)
    cache_kr = jnp.pad(cache_kr, ((0, 0), (0, 0), (QK_NOPE, LANES - QK_HEAD)))
    rope = _rope_tables(dec_seq // GRID_W)
    y_sample, _ = _trunk(x_sample.reshape(dec_batch * dec_seq, D_MODEL), mods, gains, ffn_w, mla_w,
                         pool_wb, pool_sc, (cache_lat, cache_kr, past), rope, seq=dec_seq,
                         mod_row0=1, mod_seq=dec_seq)
    return (y_prompt.reshape(batch, seq, D_MODEL), y_sample.reshape(dec_batch, dec_seq, D_MODEL),
            new_ckv, new_krope)
```

```python
import functools

import jax
import jax.numpy as jnp
from jax import lax
from jax.experimental import pallas as pl
from jax.experimental.pallas import tpu as pltpu

D_MODEL = 1024
DEPTH = 4
N_MOD = 9
FFN_HIDDEN = 2816
EPS = 1e-6
MLA_HEADS = 8
Q_LORA = 256
KV_LORA = 128
QK_NOPE = 64
QK_ROPE = 32
V_HEAD = 64
QK_HEAD = QK_NOPE + QK_ROPE
GRID_W = 64
ROPE_BASE = 10000.0
LOG2_E = 1.4426950408889634
GMLP_GROUPS = 4
GMLP_CHUNK = 128
GMLP_WIDTH = 512
POOL_WINDOWS = (2, 4, 8, 16)
POOL_GROUP_CH = D_MODEL // len(POOL_WINDOWS)
POOL_HALO = max(POOL_WINDOWS) // 2
POOL_BLOCK = 128

LANES = 128
SUBLANES = 8
MXU_TILE = 256
HEAD_W = MLA_HEADS * LANES
VT_W = MLA_HEADS * V_HEAD
PROJ_W = Q_LORA + KV_LORA + 2 * LANES + 2 * GMLP_WIDTH
VMEM_LIMIT = 56 * 1024 * 1024

TOKEN_TILE = 512
FFN_TILE = 1024
ATTN_Q_TILE = 256
ATTN_KEY_CHUNK = 512
ATTN_TILES_IN_FLIGHT = 3

BF16 = jnp.bfloat16
F32 = jnp.float32


def _cparams(n_axes):
    return pltpu.CompilerParams(
        dimension_semantics=("arbitrary",) * n_axes, vmem_limit_bytes=VMEM_LIMIT)


def _rms(x, width):
    ss = jnp.sum(x * x, axis=-1, keepdims=True) * (1.0 / width)
    return x * lax.rsqrt(ss + EPS)


def _modulated(x, g_ref, mod_ref, k):
    shift = mod_ref[0, 3 * k:3 * k + 1, :]
    scale = mod_ref[0, 3 * k + 1:3 * k + 2, :]
    return (_rms(x, D_MODEL) * g_ref[...]) * (1.0 + scale) + shift


def _mod_spec(cond, tm):
    layer, row0, mod_seq = cond
    tiles_per_row = mod_seq // tm
    return pl.BlockSpec((None, 1, N_MOD, D_MODEL), lambda i: (layer, row0 + i // tiles_per_row, 0, 0))


def _gain_spec(layer, k):
    return pl.BlockSpec((None, 1, D_MODEL), lambda i: (3 * layer + k, 0, 0))


def _resident(shape, index):
    return pl.BlockSpec(shape, lambda i: index, pipeline_mode=pl.Buffered(1))


def _dot(a, b):
    return jnp.dot(a, b, preferred_element_type=F32)


def _dot_nt(a, b):
    return lax.dot_general(a, b, (((1,), (1,)), ((), ())), preferred_element_type=F32)


def _mod_kernel(c_ref, w_ref, b_ref, o_ref):
    c = c_ref[...]
    a = (c * (1.0 / (1.0 + jnp.exp(-c)))).astype(BF16)
    o_ref[0] = _dot(a, w_ref[0].astype(BF16)) + b_ref[0]


def _modulation(cvecs, w_mod, b_mod):
    rows = cvecs.shape[0]
    tn = D_MODEL
    n_out = N_MOD * D_MODEL
    return pl.pallas_call(
        _mod_kernel,
        grid=(DEPTH, n_out // tn),
        in_specs=[
            pl.BlockSpec((rows, D_MODEL), lambda i, j: (0, 0)),
            pl.BlockSpec((1, D_MODEL, tn), lambda i, j: (i, 0, j)),
            pl.BlockSpec((1, 1, tn), lambda i, j: (i, 0, j)),
        ],
        out_specs=pl.BlockSpec((1, rows, tn), lambda i, j: (i, 0, j)),
        out_shape=jax.ShapeDtypeStruct((DEPTH, rows, n_out), F32),
        compiler_params=_cparams(2),
        name="modulation",
    )(cvecs, w_mod, b_mod.reshape(DEPTH, 1, n_out))


def _pool_mix(h, prev_h, next_h, pw_ref, ps_ref, hi_buf, lo_buf, *, seq, pos0):
    rows = h.shape[0]
    seg = min(rows, seq)
    whole = seg == seq
    lead = 2 * POOL_HALO
    win = 2 * POOL_BLOCK
    t_idx = lax.broadcasted_iota(jnp.int32, (POOL_BLOCK, win), 0)
    u_idx = lax.broadcasted_iota(jnp.int32, (POOL_BLOCK, win), 1)
    bands = [((u_idx >= t_idx + lead - w // 2) & (u_idx < t_idx + lead + w // 2)).astype(BF16)
             for w in POOL_WINDOWS]
    pos = (0 if whole else pos0) + lax.broadcasted_iota(jnp.int32, (seg, 1), 0)
    inv_counts = [1.0 / (jnp.minimum(pos + w // 2, seq) - jnp.maximum(pos - w // 2, 0)).astype(F32)
                  for w in POOL_WINDOWS]
    pad = jnp.zeros((POOL_HALO, D_MODEL), F32)
    tail = jnp.zeros((POOL_BLOCK - lead - POOL_HALO, D_MODEL), F32)
    mixes = []
    for s in range(rows // seg):
        hs = h[s * seg:(s + 1) * seg, :]
        ext = jnp.concatenate([pad, pad if whole else prev_h, hs, pad if whole else next_h, tail], axis=0)
        hi = ext.astype(BF16)
        hi_buf[...] = hi
        lo_buf[...] = (ext - hi.astype(F32)).astype(BF16)
        outs = []
        for gi in range(len(POOL_WINDOWS)):
            cols = slice(gi * POOL_GROUP_CH, (gi + 1) * POOL_GROUP_CH)
            diffs = []
            for b in range(seg // POOL_BLOCK):
                out_rows = slice(b * POOL_BLOCK, (b + 1) * POOL_BLOCK)
                win_rows = slice(b * POOL_BLOCK, b * POOL_BLOCK + win)
                total = _dot(bands[gi], hi_buf[win_rows, cols]) + _dot(bands[gi], lo_buf[win_rows, cols])
                mean = total * inv_counts[gi][out_rows, :]
                diffs.append((mean - hs[out_rows, cols]).astype(BF16))
            outs.append(_dot(jnp.concatenate(diffs, axis=0), pw_ref[gi]))
        mixes.append(jnp.concatenate(outs, axis=-1))
    mix = mixes[0] if len(mixes) == 1 else jnp.concatenate(mixes, axis=0)
    return mix * ps_ref[...]


def _ffn_kernel(*refs, k, pre, seq):
    refs = list(refs)
    x_ref = refs.pop(0)
    if pre == "mix":
        attn_ref, gout_ref, wa_ref, wg_ref = refs[:4]
        refs = refs[4:]
    elif pre == "pool":
        xp_ref, xn_ref, gp_ref, pw_ref, ps_ref = refs[:5]
        refs = refs[5:]
    mod_ref, g_ref, w1_ref, w3_ref, w2_ref, o_ref, gated_ref = refs[:7]

    x = x_ref[...]
    if pre == "mix":
        mix = _dot(attn_ref[...], wa_ref[...]) + _dot(gout_ref[...], wg_ref[...])
        x = x + mod_ref[0, 5:6, :] * mix
    elif pre == "pool":
        tm = x.shape[0]
        tiles = max(seq // tm, 1)
        j = pl.program_id(0) % tiles
        h = _modulated(x, gp_ref, mod_ref, 1)
        prev_h = _modulated(xp_ref[...], gp_ref, mod_ref, 1) * (j != 0).astype(F32)
        next_h = _modulated(xn_ref[...], gp_ref, mod_ref, 1) * (j != tiles - 1).astype(F32)
        mix = _pool_mix(h, prev_h, next_h, pw_ref, ps_ref, refs[7], refs[8], seq=seq, pos0=j * tm)
        x = x + mod_ref[0, 5:6, :] * mix

    hb = _modulated(x, g_ref, mod_ref, k).astype(BF16)
    for c in range(FFN_HIDDEN // MXU_TILE):
        cols = slice(c * MXU_TILE, (c + 1) * MXU_TILE)
        a = _dot(hb, w1_ref[:, cols])
        b = _dot(hb, w3_ref[:, cols])
        gated_ref[:, cols] = (a * (1.0 / (1.0 + jnp.exp(-a))) * b).astype(BF16)
    gate = mod_ref[0, 3 * k + 2:3 * k + 3, :]
    o_ref[...] = x + (0.5 * gate) * _dot(gated_ref[...], w2_ref[...])


def _ffn(x, mods, gains, ffn_w, *, cond, k, tm, pre=None, pre_args=(), seq=None):
    t = x.shape[0]
    layer = cond[0]
    which = 0 if k == 0 else 1
    tok = lambda w: pl.BlockSpec((tm, w), lambda i: (i, 0))
    in_specs, args, scratch = [tok(D_MODEL)], [x], [pltpu.VMEM((tm, FFN_HIDDEN), BF16)]
    if pre == "mix":
        attn, gout, w_out_a, w_out_g = pre_args
        in_specs += [tok(HEAD_W), tok(GMLP_WIDTH), _resident((HEAD_W, D_MODEL), (0, 0)),
                     _resident((GMLP_WIDTH, D_MODEL), (0, 0))]
        args += [attn, gout, w_out_a, w_out_g]
    elif pre == "pool":
        pool_w, pool_scale, j = pre_args
        halo_blocks, n_halo = tm // POOL_HALO, t // POOL_HALO
        in_specs += [
            pl.BlockSpec((POOL_HALO, D_MODEL), lambda i: (jnp.maximum(i * halo_blocks - 1, 0), 0)),
            pl.BlockSpec((POOL_HALO, D_MODEL), lambda i: (jnp.minimum((i + 1) * halo_blocks, n_halo - 1), 0)),
            _gain_spec(layer, 1),
            _resident((None, len(POOL_WINDOWS), POOL_GROUP_CH, POOL_GROUP_CH), (j, 0, 0, 0)),
            _resident((None, 1, D_MODEL), (j, 0, 0)),
        ]
        args += [x, x, gains, pool_w, pool_scale]
        assert min(tm, seq) % POOL_BLOCK == 0
        scratch += [pltpu.VMEM((min(tm, seq) + POOL_BLOCK, D_MODEL), BF16)] * 2
    w1, w3, w2 = ffn_w
    in_specs += [
        _mod_spec(cond, tm),
        _gain_spec(layer, k),
        _resident((None, None, D_MODEL, FFN_HIDDEN), (layer, which, 0, 0)),
        _resident((None, None, D_MODEL, FFN_HIDDEN), (layer, which, 0, 0)),
        _resident((None, None, FFN_HIDDEN, D_MODEL), (layer, which, 0, 0)),
    ]
    args += [mods, gains, w1, w3, w2]
    return pl.pallas_call(
        functools.partial(_ffn_kernel, k=k, pre=pre, seq=seq),
        grid=(t // tm,),
        in_specs=in_specs,
        out_specs=tok(D_MODEL),
        out_shape=jax.ShapeDtypeStruct(x.shape, F32),
        scratch_shapes=scratch,
        compiler_params=_cparams(1),
        name="ffn" if pre is None else pre + "_ffn",
    )(*args)


def _head_gains(gain, gain_partner, rope):
    if rope is None:
        return gain, None
    cos, sin = rope
    return gain * cos, gain_partner * sin


def _norm_rope(x, partner, gains):
    ss = jnp.sum(x * x, axis=-1, keepdims=True) * (1.0 / QK_HEAD)
    y = x * gains[0]
    if gains[1] is not None:
        y = y + partner * gains[1]
    return y * lax.rsqrt(ss + EPS)


def _keys_values(kv_lat, krope_blk, krope_partner, w_kvb_ref, w_kvbt_ref, gains, k_ref, vt_ref):
    vt_ref[...] = _dot_nt(w_kvbt_ref[...], kv_lat).astype(BF16)
    kv = _dot(kv_lat, w_kvb_ref[...])
    nope_lane = lax.broadcasted_iota(jnp.int32, (1, LANES), 1) < QK_NOPE
    for h in range(MLA_HEADS):
        blk = slice(h * LANES, (h + 1) * LANES)
        kh = jnp.where(nope_lane, kv[:, blk], krope_blk)
        k_ref[:, blk] = _norm_rope(kh, krope_partner, gains).astype(BF16)


def _gelu_tanh(x):
    return 0.5 * x * (1.0 + jnp.tanh(0.7978845608028654 * (x + 0.044715 * (x * x * x))))


def _mla_pre_kernel(*refs, roped, emit_latents):
    (x_ref, mod_ref, g_ref, w_in_ref, qa_ref, kva_ref, w_qb_ref, w_kvb_ref, w_kvbt_ref, qn_ref, kn_ref,
     gv_ref, ws_ref, gb_ref) = refs[:14]
    refs = refs[14:]
    rope = None
    if roped:
        w_qbp_ref = refs[0]
        rope = (refs[1][...], refs[2][...])
        refs = refs[3:]
    q_ref, k_ref, vt_ref, gout_ref = refs[:4]

    hb = _modulated(x_ref[...], g_ref, mod_ref, 1).astype(BF16)
    proj = _dot(hb, w_in_ref[...])
    o_kv, o_kr = Q_LORA, Q_LORA + KV_LORA
    o_kp, o_u = o_kr + LANES, o_kr + 2 * LANES
    o_v = o_u + GMLP_WIDTH

    q_lat = (_rms(proj[:, :o_kv], Q_LORA) * qa_ref[...]).astype(BF16)
    q = _dot(q_lat, w_qb_ref[...])
    q_partner = _dot(q_lat, w_qbp_ref[...]) if roped else None
    q_gains = _head_gains(qn_ref[0:1, :] * (QK_HEAD ** -0.5 * LOG2_E),
                          qn_ref[1:2, :] * (QK_HEAD ** -0.5 * LOG2_E), rope)
    for h in range(MLA_HEADS):
        blk = slice(h * LANES, (h + 1) * LANES)
        qp = q_partner[:, blk] if roped else None
        q_ref[:, blk] = _norm_rope(q[:, blk], qp, q_gains).astype(BF16)

    kv_lat = _rms(proj[:, o_kv:o_kr], KV_LORA) * kva_ref[...]
    krope_blk = proj[:, o_kr:o_kp]
    k_gains = _head_gains(kn_ref[0:1, :], kn_ref[1:2, :], rope)
    _keys_values(kv_lat.astype(BF16), krope_blk, proj[:, o_kp:o_u], w_kvb_ref, w_kvbt_ref, k_gains,
                 k_ref, vt_ref)
    if emit_latents:
        refs[4][...] = kv_lat
        refs[5][...] = krope_blk

    u = _gelu_tanh(proj[:, o_u:o_v])
    v = (_rms(_gelu_tanh(proj[:, o_v:]), GMLP_WIDTH) * gv_ref[...]).astype(BF16)
    for n in range(x_ref.shape[0] // GMLP_CHUNK):
        rows = slice(n * GMLP_CHUNK, (n + 1) * GMLP_CHUNK)
        for grp in range(GMLP_GROUPS):
            cols = slice(grp * LANES, (grp + 1) * LANES)
            mixed = _dot(ws_ref[grp], v[rows, cols]) + gb_ref[:, cols]
            gout_ref[rows, cols] = (u[rows, cols] * mixed).astype(BF16)


def _mla_pre(x, mods, gains, wts, rope, *, cond, seq, tm, emit_latents):
    t = x.shape[0]
    per_seq = seq // tm
    const = lambda i: (0, 0)
    tok = lambda w: pl.BlockSpec((tm, w), lambda i: (i, 0))
    in_specs = [
        tok(D_MODEL),
        _mod_spec(cond, tm),
        _gain_spec(cond[0], 1),
        pl.BlockSpec((D_MODEL, PROJ_W), const),
        pl.BlockSpec((1, Q_LORA), const),
        pl.BlockSpec((1, KV_LORA), const),
        pl.BlockSpec((Q_LORA, HEAD_W), const),
        pl.BlockSpec((KV_LORA, HEAD_W), const),
        pl.BlockSpec((VT_W, KV_LORA), const),
        pl.BlockSpec((2, LANES), const),
        pl.BlockSpec((2, LANES), const),
        pl.BlockSpec((1, GMLP_WIDTH), const),
        pl.BlockSpec((GMLP_GROUPS, GMLP_CHUNK, GMLP_CHUNK), lambda i: (0, 0, 0)),
        pl.BlockSpec((GMLP_CHUNK, GMLP_WIDTH), const),
    ]
    args = [x, mods, gains, wts["w_in"], wts["q_a_norm"], wts["kv_a_norm"],
            wts["w_qb"], wts["w_kvb"], wts["w_kvbt"], wts["q_norm"], wts["k_norm"], wts["gmlp_v_norm"],
            wts["gmlp_ws"], wts["gmlp_b"]]
    if rope is not None:
        in_specs += [pl.BlockSpec((Q_LORA, HEAD_W), const)]
        in_specs += [pl.BlockSpec((tm, LANES), lambda i: (i % per_seq, 0))] * 2
        args += [wts["w_qb_partner"]] + list(rope)
    out_specs = [tok(HEAD_W), tok(HEAD_W), pl.BlockSpec((VT_W, tm), lambda i: (0, i)), tok(GMLP_WIDTH)]
    out_shape = [jax.ShapeDtypeStruct((t, HEAD_W), BF16)] * 2 + [
        jax.ShapeDtypeStruct((VT_W, t), BF16), jax.ShapeDtypeStruct((t, GMLP_WIDTH), BF16)]
    if emit_latents:
        out_specs += [tok(KV_LORA), tok(LANES)]
        out_shape += [jax.ShapeDtypeStruct((t, KV_LORA), F32), jax.ShapeDtypeStruct((t, LANES), F32)]
    return pl.pallas_call(
        functools.partial(_mla_pre_kernel, roped=rope is not None, emit_latents=emit_latents),
        grid=(t // tm,),
        in_specs=in_specs,
        out_specs=out_specs,
        out_shape=out_shape,
        compiler_params=_cparams(1),
        name="mla_pre",
    )(*args)


def _cache_kv_kernel(lat_ref, kr_ref, w_kvb_ref, w_kvbt_ref, kn_ref, k_ref, vt_ref):
    gains = _head_gains(kn_ref[0:1, :], None, None)
    _keys_values(lat_ref[...].astype(BF16), kr_ref[...], None, w_kvb_ref, w_kvbt_ref, gains, k_ref, vt_ref)


def _cache_kv(lat, krope_blk, w_kvb, w_kvbt, k_norm, *, tm):
    t = lat.shape[0]
    const = lambda i: (0, 0)
    tok = lambda w: pl.BlockSpec((tm, w), lambda i: (i, 0))
    return pl.pallas_call(
        _cache_kv_kernel,
        grid=(t // tm,),
        in_specs=[tok(KV_LORA), tok(LANES), pl.BlockSpec((KV_LORA, HEAD_W), const),
                  pl.BlockSpec((VT_W, KV_LORA), const), pl.BlockSpec((2, LANES), const)],
        out_specs=[tok(HEAD_W), pl.BlockSpec((VT_W, tm), lambda i: (0, i))],
        out_shape=[jax.ShapeDtypeStruct((t, HEAD_W), BF16), jax.ShapeDtypeStruct((VT_W, t), BF16)],
        compiler_params=_cparams(1),
        name="cache_kv",
    )(lat, krope_blk, w_kvb, w_kvbt, k_norm)


def _sublane_groups(x):
    return x.reshape(x.shape[0] // SUBLANES, SUBLANES, x.shape[1])


def _value_block(ot):
    return jnp.concatenate([jnp.zeros((LANES - V_HEAD, ot.shape[1]), F32), ot], axis=0)


def _attn_ctx_kernel(q_ref, k_ref, vt_ref, o_ref):
    heads = [slice(h * LANES, (h + 1) * LANES) for h in range(MLA_HEADS)]
    st = jnp.concatenate([_dot_nt(k_ref[:, blk], q_ref[:, blk]) for blk in heads], axis=1)
    pt = jnp.exp2(st - jnp.max(st, axis=0, keepdims=True))
    inv = 1.0 / jnp.sum(pt, axis=0, keepdims=True)
    pt = pt.astype(BF16)
    seq = q_ref.shape[0]
    for h, blk in enumerate(heads):
        cols = slice(h * seq, (h + 1) * seq)
        ot = _dot(vt_ref[h * V_HEAD:(h + 1) * V_HEAD, :], pt[:, cols]) * inv[:, cols]
        o_ref[:, blk] = _value_block(ot).T.astype(BF16)


def _attn_ctx(q, k, vt, *, seq):
    t = q.shape[0]
    spec = pl.BlockSpec((seq, HEAD_W), lambda b: (b, 0))
    return pl.pallas_call(
        _attn_ctx_kernel,
        grid=(t // seq,),
        in_specs=[spec, spec, pl.BlockSpec((VT_W, seq), lambda b: (0, b))],
        out_specs=spec,
        out_shape=jax.ShapeDtypeStruct((t, HEAD_W), BF16),
        compiler_params=_cparams(1),
        name="attn_ctx",
    )(q, k, vt)


def _attn_dec_kernel(q_ref, k_ref, vt_ref, kc_ref, vct_ref, o_ref, *scratch, tq, ck):
    seq, past = k_ref.shape[0], kc_ref.shape[0]
    n = seq // tq
    depth = len(scratch) // 2
    lag = depth - 1
    bufs = list(zip(scratch[:depth], scratch[depth:]))
    key_blocks = [(k_ref, vt_ref, c * ck, c * ck, ck) for c in range(seq // ck)]
    key_blocks.append((kc_ref, vct_ref, 0, seq, past))

    def tile_rows(i):
        return pl.ds(pl.multiple_of(i * tq, tq), tq)

    def step(nxt, cur):
        if nxt is not None:
            q = q_ref[tile_rows(nxt[0]), :]
            mx = None
        if cur is not None:
            m = cur[2][...]
            l = jnp.zeros((SUBLANES, tq), F32)
            acc = jnp.zeros((V_HEAD, tq), F32)
        for keys, values_t, r0, s0, nr in key_blocks:
            if nxt is not None:
                st = _dot_nt(keys[r0:r0 + nr, :], q)
                nxt[1][s0:s0 + nr, :] = st
                blk_max = jnp.max(_sublane_groups(st), axis=0)
                mx = blk_max if mx is None else jnp.maximum(mx, blk_max)
            if cur is not None:
                p = jnp.exp2(_sublane_groups(cur[1][s0:s0 + nr, :]) - m[None])
                l = l + jnp.sum(p, axis=0)
                acc = acc + _dot(values_t[:, r0:r0 + nr], p.reshape(nr, tq).astype(BF16))
        if nxt is not None:
            nxt[2][...] = jnp.broadcast_to(jnp.max(mx, axis=0, keepdims=True), (SUBLANES, tq))
        if cur is not None:
            inv = 1.0 / jnp.sum(l, axis=0, keepdims=True)
            o_ref[tile_rows(cur[0]), :] = _value_block(acc * inv).T.astype(BF16)

    tile = lambda i, slot: (i,) + bufs[slot % depth]
    for i in range(lag):
        step(tile(i, i), None)

    def body(j, carry):
        for r in range(depth):
            step(tile(depth * j + r + lag, r + lag), tile(depth * j + r, r))
        return carry

    full = (n - lag) // depth
    lax.fori_loop(0, full, body, 0)
    for i in range(full * depth, n - lag):
        step(tile(i + lag, i + lag), tile(i, i))
    for i in range(n - lag, n):
        step(None, tile(i, i))


def _attn_dec(q, k, vt, kc, vct, *, seq, past, tq, ck):
    t = q.shape[0]
    assert seq % tq == 0 and seq // tq >= ATTN_TILES_IN_FLIGHT and seq % ck == 0
    cur = pl.BlockSpec((seq, LANES), lambda b, h: (b, h))
    cur_t = pl.BlockSpec((V_HEAD, seq), lambda b, h: (h, b))
    old = pl.BlockSpec((past, LANES), lambda b, h: (b, h))
    old_t = pl.BlockSpec((V_HEAD, past), lambda b, h: (h, b))
    score_buf = pltpu.VMEM((seq + past, tq), F32)
    max_buf = pltpu.VMEM((SUBLANES, tq), F32)
    return pl.pallas_call(
        functools.partial(_attn_dec_kernel, tq=tq, ck=ck),
        grid=(t // seq, MLA_HEADS),
        in_specs=[cur, cur, cur_t, old, old_t],
        out_specs=cur,
        out_shape=jax.ShapeDtypeStruct((t, HEAD_W), BF16),
        scratch_shapes=[score_buf] * ATTN_TILES_IN_FLIGHT + [max_buf] * ATTN_TILES_IN_FLIGHT,
        compiler_params=_cparams(2),
        name="attn_dec",
    )(q, k, vt, kc, vct)


def _head_blocks(w, width):
    rows = w.shape[0]
    w = w.reshape(rows, MLA_HEADS, width)
    return jnp.pad(w, ((0, 0), (0, 0), (0, LANES - width))).reshape(rows, HEAD_W)


def _rope_partner(v):
    quarter = QK_ROPE // 4
    return v.reshape(v.shape[:-1] + (2, 2, quarter))[..., ::-1, :].reshape(v.shape)


def _rope_lanes(v, fill=0.0):
    pad = [(0, 0)] * (v.ndim - 1) + [(QK_NOPE, LANES - QK_HEAD)]
    return jnp.pad(v, pad, constant_values=fill)


def _mla_weights(j, w_in, q_a_norm, kv_a_norm, w_qb, w_kvb, q_norm, k_norm, gmlp_v_norm, gmlp_ws,
                 gmlp_b, w_out):
    o_kv, o_kr, o_g = Q_LORA, Q_LORA + KV_LORA, Q_LORA + KV_LORA + QK_ROPE
    w_krope = w_in[j][:, o_kr:o_g]
    w_in_p = jnp.concatenate([w_in[j][:, :o_kr], _rope_lanes(w_krope), _rope_lanes(_rope_partner(w_krope)),
                              w_in[j][:, o_g:]], axis=1)
    gain_rows = lambda g: jnp.stack([jnp.pad(g, (0, LANES - QK_HEAD)), _rope_lanes(_rope_partner(g[QK_NOPE:]))])
    w_q_rope = w_qb[j].reshape(Q_LORA, MLA_HEADS, QK_HEAD)[:, :, QK_NOPE:]
    w_out_a = w_out[j][:MLA_HEADS * V_HEAD].reshape(MLA_HEADS, V_HEAD, D_MODEL)
    w_out_a = jnp.pad(w_out_a, ((0, 0), (LANES - V_HEAD, 0), (0, 0))).reshape(HEAD_W, D_MODEL)
    return {
        "w_in": w_in_p.astype(BF16),
        "q_a_norm": q_a_norm[j].reshape(1, Q_LORA),
        "kv_a_norm": kv_a_norm[j].reshape(1, KV_LORA),
        "w_qb": _head_blocks(w_qb[j], QK_HEAD).astype(BF16),
        "w_qb_partner": _rope_lanes(_rope_partner(w_q_rope)).reshape(Q_LORA, HEAD_W).astype(BF16),
        "w_kvb": w_kvb[j].astype(BF16),
        "w_kvbt": w_kvb[j].reshape(KV_LORA, MLA_HEADS, LANES)[:, :, QK_NOPE:].reshape(KV_LORA, VT_W).T.astype(BF16),
        "q_norm": gain_rows(q_norm[j]),
        "k_norm": gain_rows(k_norm[j]),
        "gmlp_v_norm": gmlp_v_norm[j].reshape(1, GMLP_WIDTH),
        "gmlp_ws": gmlp_ws[j].astype(BF16),
        "gmlp_b": jnp.repeat(gmlp_b[j].T, LANES, axis=1),
        "w_out_a": w_out_a.astype(BF16),
        "w_out_g": w_out[j][MLA_HEADS * V_HEAD:].astype(BF16),
    }


def _rope_tables(rows):
    row = jnp.repeat(jnp.arange(rows), GRID_W).astype(F32)
    col = jnp.tile(jnp.arange(GRID_W), rows).astype(F32)
    per_axis = QK_ROPE // 2
    inv = ROPE_BASE ** (-jnp.arange(0, per_axis, 2, dtype=F32) / per_axis)
    ang_r, ang_c = row[:, None] * inv, col[:, None] * inv
    cr, sr, cc, sc = jnp.cos(ang_r), jnp.sin(ang_r), jnp.cos(ang_c), jnp.sin(ang_c)
    cos = _rope_lanes(jnp.concatenate([cr, cr, cc, cc], axis=1), fill=1.0)
    sin = _rope_lanes(jnp.concatenate([-sr, sr, -sc, sc], axis=1))
    return cos, sin


def _trunk(x, mods, gains, ffn_w, mla_w, pool_w, pool_scale, cache, rope, *, seq, mod_row0, mod_seq):
    tm = min(TOKEN_TILE, mod_seq)
    ffn_tm = min(FFN_TILE, mod_seq)
    assert ffn_tm % min(ffn_tm, seq) == 0 and max(ffn_tm, seq) % min(ffn_tm, seq) == 0
    latents = []
    for i in range(DEPTH):
        cond = (i, mod_row0, mod_seq)
        x = _ffn(x, mods, gains, ffn_w, cond=cond, k=0, tm=ffn_tm)
        j = i // 2
        if i % 2 == 0:
            wts = mla_w[j]
            outs = _mla_pre(x, mods, gains, wts, rope, cond=cond, seq=seq, tm=tm, emit_latents=cache is None)
            q, k, vt, gout = outs[:4]
            if cache is None:
                latents.append(outs[4:])
                attn = _attn_ctx(q, k, vt, seq=seq)
            else:
                lat, krope_blk, past = cache
                kc, vct = _cache_kv(lat[j], krope_blk[j], wts["w_kvb"], wts["w_kvbt"], wts["k_norm"], tm=tm)
                attn = _attn_dec(q, k, vt, kc, vct, seq=seq, past=past, tq=ATTN_Q_TILE, ck=ATTN_KEY_CHUNK)
            pre, pre_args = "mix", (attn, gout, wts["w_out_a"], wts["w_out_g"])
        else:
            pre, pre_args = "pool", (pool_w, pool_scale, j)
        x = _ffn(x, mods, gains, ffn_w, cond=cond, k=2, tm=ffn_tm, pre=pre, pre_args=pre_args, seq=seq)
    return x, latents


def kernel(x_prompt, x_sample, cache_ckv, cache_krope, c, c_ctx, w_mod, b_mod, norm_g, ffn_w1, ffn_w3,
           ffn_w2, w_in, q_a_norm, kv_a_norm, w_qb, w_kvb, q_norm, k_norm, gmlp_v_norm, gmlp_ws, gmlp_b,
           w_out, pool_w, pool_scale):
    batch, seq, _ = x_prompt.shape
    dec_batch, dec_seq, _ = x_sample.shape
    past = cache_ckv.shape[2]
    n_mla = w_in.shape[0]

    rows = -(-(1 + dec_batch) // SUBLANES) * SUBLANES
    cvecs = jnp.zeros((rows, D_MODEL), F32).at[0].set(c_ctx).at[1:1 + dec_batch].set(c)
    mods = _modulation(cvecs, w_mod, b_mod).reshape(DEPTH, rows, N_MOD, D_MODEL)

    gains = norm_g.reshape(DEPTH * 3, 1, D_MODEL)
    ffn_w = (ffn_w1.astype(BF16), ffn_w3.astype(BF16), ffn_w2.astype(BF16))
    mla_w = [_mla_weights(j, w_in, q_a_norm, kv_a_norm, w_qb, w_kvb, q_norm, k_norm, gmlp_v_norm,
                          gmlp_ws, gmlp_b, w_out) for j in range(n_mla)]
    pool_wb = pool_w.astype(BF16)
    pool_sc = pool_scale.reshape(-1, 1, D_MODEL)

    y_prompt, latents = _trunk(x_prompt.reshape(batch * seq, D_MODEL), mods, gains, ffn_w, mla_w,
                               pool_wb, pool_sc, None, None, seq=seq, mod_row0=0, mod_seq=batch * seq)
    new_ckv = jnp.stack([l[0].reshape(batch, seq, KV_LORA) for l in latents], axis=1)
    new_krope = jnp.stack(
        [l[1][:, QK_NOPE:QK_HEAD].reshape(batch, seq, QK_ROPE) for l in latents], axis=1)

    cache_lat = cache_ckv.transpose(1, 0, 2, 3).reshape(n_mla, dec_batch * past, KV_LORA)
    cache_kr = cache_krope.transpose(1, 0, 2, 3).reshape(n_mla, dec_batch * past, QK_ROPE)
    cache_kr = jnp.pad(cache_kr, ((0, 0), (0, 0), (QK_NOPE, LANES - QK_HEAD)))
    rope = _rope_tables(dec_seq // GRID_W)
    y_sample, _ = _trunk(x_sample.reshape(dec_batch * dec_seq, D_MODEL), mods, gains, ffn_w, mla_w,
                         pool_wb, pool_sc, (cache_lat, cache_kr, past), rope, seq=dec_seq,
                         mod_row0=1, mod_seq=dec_seq)
    return (y_prompt.reshape(batch, seq, D_MODEL), y_sample.reshape(dec_batch, dec_seq, D_MODEL),
            new_ckv, new_krope)
```

```python
import functools

import jax
import jax.numpy as jnp
from jax import lax
from jax.experimental import pallas as pl
from jax.experimental.pallas import tpu as pltpu

D_MODEL = 1024
DEPTH = 4
N_MOD = 9
FFN_HIDDEN = 2816
EPS = 1e-6
MLA_HEADS = 8
Q_LORA = 256
KV_LORA = 128
QK_NOPE = 64
QK_ROPE = 32
V_HEAD = 64
QK_HEAD = QK_NOPE + QK_ROPE
GRID_W = 64
ROPE_BASE = 10000.0
LOG2_E = 1.4426950408889634
GMLP_GROUPS = 4
GMLP_CHUNK = 128
GMLP_WIDTH = 512
POOL_WINDOWS = (2, 4, 8, 16)
POOL_GROUP_CH = D_MODEL // len(POOL_WINDOWS)
POOL_HALO = max(POOL_WINDOWS) // 2
POOL_BLOCK = 128

LANES = 128
SUBLANES = 8
MXU_TILE = 256
HEAD_W = MLA_HEADS * LANES
VT_W = MLA_HEADS * V_HEAD
PROJ_W = Q_LORA + KV_LORA + 2 * LANES + 2 * GMLP_WIDTH
VMEM_LIMIT = 56 * 1024 * 1024

TOKEN_TILE = 512
FFN_TILE = 1024
ATTN_Q_TILE = 256
ATTN_KEY_CHUNK = 512
ATTN_TILES_IN_FLIGHT = 3

BF16 = jnp.bfloat16
F32 = jnp.float32


def _cparams(n_axes):
    return pltpu.CompilerParams(
        dimension_semantics=("arbitrary",) * n_axes, vmem_limit_bytes=VMEM_LIMIT)


def _rms(x, width):
    ss = jnp.sum(x * x, axis=-1, keepdims=True) * (1.0 / width)
    return x * lax.rsqrt(ss + EPS)


def _modulated(x, g_ref, mod_ref, k):
    shift = mod_ref[0, 3 * k:3 * k + 1, :]
    scale = mod_ref[0, 3 * k + 1:3 * k + 2, :]
    return (_rms(x, D_MODEL) * g_ref[...]) * (1.0 + scale) + shift


def _mod_spec(cond, tm):
    layer, row0, mod_seq = cond
    tiles_per_row = mod_seq // tm
    return pl.BlockSpec((None, 1, N_MOD, D_MODEL), lambda i: (layer, row0 + i // tiles_per_row, 0, 0))


def _gain_spec(layer, k):
    return pl.BlockSpec((None, 1, D_MODEL), lambda i: (3 * layer + k, 0, 0))


def _resident(shape, index):
    return pl.BlockSpec(shape, lambda i: index, pipeline_mode=pl.Buffered(1))


def _dot(a, b):
    return jnp.dot(a, b, preferred_element_type=F32)


def _dot_nt(a, b):
    return lax.dot_general(a, b, (((1,), (1,)), ((), ())), preferred_element_type=F32)


def _mod_kernel(c_ref, w_ref, b_ref, o_ref):
    c = c_ref[...]
    a = (c * (1.0 / (1.0 + jnp.exp(-c)))).astype(BF16)
    o_ref[0] = _dot(a, w_ref[0].astype(BF16)) + b_ref[0]


def _modulation(cvecs, w_mod, b_mod):
    rows = cvecs.shape[0]
    tn = D_MODEL
    n_out = N_MOD * D_MODEL
    return pl.pallas_call(
        _mod_kernel,
        grid=(DEPTH, n_out // tn),
        in_specs=[
            pl.BlockSpec((rows, D_MODEL), lambda i, j: (0, 0)),
            pl.BlockSpec((1, D_MODEL, tn), lambda i, j: (i, 0, j)),
            pl.BlockSpec((1, 1, tn), lambda i, j: (i, 0, j)),
        ],
        out_specs=pl.BlockSpec((1, rows, tn), lambda i, j: (i, 0, j)),
        out_shape=jax.ShapeDtypeStruct((DEPTH, rows, n_out), F32),
        compiler_params=_cparams(2),
        name="modulation",
    )(cvecs, w_mod, b_mod.reshape(DEPTH, 1, n_out))


def _pool_mix(h, prev_h, next_h, pw_ref, ps_ref, hi_buf, lo_buf, *, seq, pos0):
    rows = h.shape[0]
    seg = min(rows, seq)
    whole = seg == seq
    lead = 2 * POOL_HALO
    win = 2 * POOL_BLOCK
    t_idx = lax.broadcasted_iota(jnp.int32, (POOL_BLOCK, win), 0)
    u_idx = lax.broadcasted_iota(jnp.int32, (POOL_BLOCK, win), 1)
    bands = [((u_idx >= t_idx + lead - w // 2) & (u_idx < t_idx + lead + w // 2)).astype(BF16)
             for w in POOL_WINDOWS]
    pos = (0 if whole else pos0) + lax.broadcasted_iota(jnp.int32, (seg, 1), 0)
    inv_counts = [1.0 / (jnp.minimum(pos + w // 2, seq) - jnp.maximum(pos - w // 2, 0)).astype(F32)
                  for w in POOL_WINDOWS]
    pad = jnp.zeros((POOL_HALO, D_MODEL), F32)
    tail = jnp.zeros((POOL_BLOCK - lead - POOL_HALO, D_MODEL), F32)
    mixes = []
    for s in range(rows // seg):
        hs = h[s * seg:(s + 1) * seg, :]
        ext = jnp.concatenate([pad, pad if whole else prev_h, hs, pad if whole else next_h, tail], axis=0)
        hi = ext.astype(BF16)
        hi_buf[...] = hi
        lo_buf[...] = (ext - hi.astype(F32)).astype(BF16)
        outs = []
        for gi in range(len(POOL_WINDOWS)):
            cols = slice(gi * POOL_GROUP_CH, (gi + 1) * POOL_GROUP_CH)
            diffs = []
            for b in range(seg // POOL_BLOCK):
                out_rows = slice(b * POOL_BLOCK, (b + 1) * POOL_BLOCK)
                win_rows = slice(b * POOL_BLOCK, b * POOL_BLOCK + win)
                total = _dot(bands[gi], hi_buf[win_rows, cols]) + _dot(bands[gi], lo_buf[win_rows, cols])
                mean = total * inv_counts[gi][out_rows, :]
                diffs.append((mean - hs[out_rows, cols]).astype(BF16))
            outs.append(_dot(jnp.concatenate(diffs, axis=0), pw_ref[gi]))
        mixes.append(jnp.concatenate(outs, axis=-1))
    mix = mixes[0] if len(mixes) == 1 else jnp.concatenate(mixes, axis=0)
    return mix * ps_ref[...]


def _ffn_kernel(*refs, k, pre, seq):
    refs = list(refs)
    x_ref = refs.pop(0)
    if pre == "mix":
        attn_ref, gout_ref, wa_ref, wg_ref = refs[:4]
        refs = refs[4:]
    elif pre == "pool":
        xp_ref, xn_ref, gp_ref, pw_ref, ps_ref = refs[:5]
        refs = refs[5:]
    mod_ref, g_ref, w1_ref, w3_ref, w2_ref, o_ref, gated_ref = refs[:7]

    x = x_ref[...]
    if pre == "mix":
        mix = _dot(attn_ref[...], wa_ref[...]) + _dot(gout_ref[...], wg_ref[...])
        x = x + mod_ref[0, 5:6, :] * mix
    elif pre == "pool":
        tm = x.shape[0]
        tiles = max(seq // tm, 1)
        j = pl.program_id(0) % tiles
        h = _modulated(x, gp_ref, mod_ref, 1)
        prev_h = _modulated(xp_ref[...], gp_ref, mod_ref, 1) * (j != 0).astype(F32)
        next_h = _modulated(xn_ref[...], gp_ref, mod_ref, 1) * (j != tiles - 1).astype(F32)
        mix = _pool_mix(h, prev_h, next_h, pw_ref, ps_ref, refs[7], refs[8], seq=seq, pos0=j * tm)
        x = x + mod_ref[0, 5:6, :] * mix

    hb = _modulated(x, g_ref, mod_ref, k).astype(BF16)
    for c in range(FFN_HIDDEN // MXU_TILE):
        cols = slice(c * MXU_TILE, (c + 1) * MXU_TILE)
        a = _dot(hb, w1_ref[:, cols])
        b = _dot(hb, w3_ref[:, cols])
        gated_ref[:, cols] = (a * (1.0 / (1.0 + jnp.exp(-a))) * b).astype(BF16)
    gate = mod_ref[0, 3 * k + 2:3 * k + 3, :]
    o_ref[...] = x + (0.5 * gate) * _dot(gated_ref[...], w2_ref[...])


def _ffn(x, mods, gains, ffn_w, *, cond, k, tm, pre=None, pre_args=(), seq=None):
    t = x.shape[0]
    layer = cond[0]
    which = 0 if k == 0 else 1
    tok = lambda w: pl.BlockSpec((tm, w), lambda i: (i, 0))
    in_specs, args, scratch = [tok(D_MODEL)], [x], [pltpu.VMEM((tm, FFN_HIDDEN), BF16)]
    if pre == "mix":
        attn, gout, w_out_a, w_out_g = pre_args
        in_specs += [tok(VT_W), tok(GMLP_WIDTH), _resident((VT_W, D_MODEL), (0, 0)),
                     _resident((GMLP_WIDTH, D_MODEL), (0, 0))]
        args += [attn, gout, w_out_a, w_out_g]
    elif pre == "pool":
        pool_w, pool_scale, j = pre_args
        halo_blocks, n_halo = tm // POOL_HALO, t // POOL_HALO
        in_specs += [
            pl.BlockSpec((POOL_HALO, D_MODEL), lambda i: (jnp.maximum(i * halo_blocks - 1, 0), 0)),
            pl.BlockSpec((POOL_HALO, D_MODEL), lambda i: (jnp.minimum((i + 1) * halo_blocks, n_halo - 1), 0)),
            _gain_spec(layer, 1),
            _resident((None, len(POOL_WINDOWS), POOL_GROUP_CH, POOL_GROUP_CH), (j, 0, 0, 0)),
            _resident((None, 1, D_MODEL), (j, 0, 0)),
        ]
        args += [x, x, gains, pool_w, pool_scale]
        assert min(tm, seq) % POOL_BLOCK == 0
        scratch += [pltpu.VMEM((min(tm, seq) + POOL_BLOCK, D_MODEL), BF16)] * 2
    w1, w3, w2 = ffn_w
    in_specs += [
        _mod_spec(cond, tm),
        _gain_spec(layer, k),
        _resident((None, None, D_MODEL, FFN_HIDDEN), (layer, which, 0, 0)),
        _resident((None, None, D_MODEL, FFN_HIDDEN), (layer, which, 0, 0)),
        _resident((None, None, FFN_HIDDEN, D_MODEL), (layer, which, 0, 0)),
    ]
    args += [mods, gains, w1, w3, w2]
    return pl.pallas_call(
        functools.partial(_ffn_kernel, k=k, pre=pre, seq=seq),
        grid=(t // tm,),
        in_specs=in_specs,
        out_specs=tok(D_MODEL),
        out_shape=jax.ShapeDtypeStruct(x.shape, F32),
        scratch_shapes=scratch,
        compiler_params=_cparams(1),
        name="ffn" if pre is None else pre + "_ffn",
    )(*args)


def _head_gains(gain, gain_partner, rope):
    if rope is None:
        return gain, None
    cos, sin = rope
    return gain * cos, gain_partner * sin


def _norm_rope(x, partner, gains):
    ss = jnp.sum(x * x, axis=-1, keepdims=True) * (1.0 / QK_HEAD)
    y = x * gains[0]
    if gains[1] is not None:
        y = y + partner * gains[1]
    return y * lax.rsqrt(ss + EPS)


def _keys_values(kv_lat, krope_blk, krope_partner, w_kvb_ref, w_kvbt_ref, gains, k_ref, vt_ref):
    vt_ref[...] = _dot_nt(w_kvbt_ref[...], kv_lat).astype(BF16)
    kv = _dot(kv_lat, w_kvb_ref[...])
    nope_lane = lax.broadcasted_iota(jnp.int32, (1, LANES), 1) < QK_NOPE
    for h in range(MLA_HEADS):
        blk = slice(h * LANES, (h + 1) * LANES)
        kh = jnp.where(nope_lane, kv[:, blk], krope_blk)
        k_ref[:, blk] = _norm_rope(kh, krope_partner, gains).astype(BF16)


def _gelu_tanh(x):
    return 0.5 * x * (1.0 + jnp.tanh(0.7978845608028654 * (x + 0.044715 * (x * x * x))))


def _mla_pre_kernel(*refs, roped, emit_latents):
    (x_ref, mod_ref, g_ref, w_in_ref, qa_ref, kva_ref, w_qb_ref, w_kvb_ref, w_kvbt_ref, qn_ref, kn_ref,
     gv_ref, ws_ref, gb_ref) = refs[:14]
    refs = refs[14:]
    rope = None
    if roped:
        w_qbp_ref = refs[0]
        rope = (refs[1][...], refs[2][...])
        refs = refs[3:]
    q_ref, k_ref, vt_ref, gout_ref = refs[:4]

    hb = _modulated(x_ref[...], g_ref, mod_ref, 1).astype(BF16)
    proj = _dot(hb, w_in_ref[...])
    o_kv, o_kr = Q_LORA, Q_LORA + KV_LORA
    o_kp, o_u = o_kr + LANES, o_kr + 2 * LANES
    o_v = o_u + GMLP_WIDTH

    q_lat = (_rms(proj[:, :o_kv], Q_LORA) * qa_ref[...]).astype(BF16)
    q = _dot(q_lat, w_qb_ref[...])
    q_partner = _dot(q_lat, w_qbp_ref[...]) if roped else None
    q_gains = _head_gains(qn_ref[0:1, :] * (QK_HEAD ** -0.5 * LOG2_E),
                          qn_ref[1:2, :] * (QK_HEAD ** -0.5 * LOG2_E), rope)
    for h in range(MLA_HEADS):
        blk = slice(h * LANES, (h + 1) * LANES)
        qp = q_partner[:, blk] if roped else None
        q_ref[:, blk] = _norm_rope(q[:, blk], qp, q_gains).astype(BF16)

    kv_lat = _rms(proj[:, o_kv:o_kr], KV_LORA) * kva_ref[...]
    krope_blk = proj[:, o_kr:o_kp]
    k_gains = _head_gains(kn_ref[0:1, :], kn_ref[1:2, :], rope)
    _keys_values(kv_lat.astype(BF16), krope_blk, proj[:, o_kp:o_u], w_kvb_ref, w_kvbt_ref, k_gains,
                 k_ref, vt_ref)
    if emit_latents:
        refs[4][...] = kv_lat
        refs[5][...] = krope_blk

    u = _gelu_tanh(proj[:, o_u:o_v])
    v = (_rms(_gelu_tanh(proj[:, o_v:]), GMLP_WIDTH) * gv_ref[...]).astype(BF16)
    for n in range(x_ref.shape[0] // GMLP_CHUNK):
        rows = slice(n * GMLP_CHUNK, (n + 1) * GMLP_CHUNK)
        for grp in range(GMLP_GROUPS):
            cols = slice(grp * LANES, (grp + 1) * LANES)
            mixed = _dot(ws_ref[grp], v[rows, cols]) + gb_ref[:, cols]
            gout_ref[rows, cols] = (u[rows, cols] * mixed).astype(BF16)


def _mla_pre(x, mods, gains, wts, rope, *, cond, seq, tm, emit_latents):
    t = x.shape[0]
    per_seq = seq // tm
    const = lambda i: (0, 0)
    tok = lambda w: pl.BlockSpec((tm, w), lambda i: (i, 0))
    in_specs = [
        tok(D_MODEL),
        _mod_spec(cond, tm),
        _gain_spec(cond[0], 1),
        pl.BlockSpec((D_MODEL, PROJ_W), const),
        pl.BlockSpec((1, Q_LORA), const),
        pl.BlockSpec((1, KV_LORA), const),
        pl.BlockSpec((Q_LORA, HEAD_W), const),
        pl.BlockSpec((KV_LORA, HEAD_W), const),
        pl.BlockSpec((HEAD_W, KV_LORA), const),
        pl.BlockSpec((2, LANES), const),
        pl.BlockSpec((2, LANES), const),
        pl.BlockSpec((1, GMLP_WIDTH), const),
        pl.BlockSpec((GMLP_GROUPS, GMLP_CHUNK, GMLP_CHUNK), lambda i: (0, 0, 0)),
        pl.BlockSpec((GMLP_CHUNK, GMLP_WIDTH), const),
    ]
    args = [x, mods, gains, wts["w_in"], wts["q_a_norm"], wts["kv_a_norm"],
            wts["w_qb"], wts["w_kvb"], wts["w_kvbt"], wts["q_norm"], wts["k_norm"], wts["gmlp_v_norm"],
            wts["gmlp_ws"], wts["gmlp_b"]]
    if rope is not None:
        in_specs += [pl.BlockSpec((Q_LORA, HEAD_W), const)]
        in_specs += [pl.BlockSpec((tm, LANES), lambda i: (i % per_seq, 0))] * 2
        args += [wts["w_qb_partner"]] + list(rope)
    out_specs = [tok(HEAD_W), tok(HEAD_W), pl.BlockSpec((HEAD_W, tm), lambda i: (0, i)), tok(GMLP_WIDTH)]
    out_shape = [jax.ShapeDtypeStruct((t, HEAD_W), BF16)] * 2 + [
        jax.ShapeDtypeStruct((HEAD_W, t), BF16), jax.ShapeDtypeStruct((t, GMLP_WIDTH), BF16)]
    if emit_latents:
        out_specs += [tok(KV_LORA), tok(LANES)]
        out_shape += [jax.ShapeDtypeStruct((t, KV_LORA), F32), jax.ShapeDtypeStruct((t, LANES), F32)]
    return pl.pallas_call(
        functools.partial(_mla_pre_kernel, roped=rope is not None, emit_latents=emit_latents),
        grid=(t // tm,),
        in_specs=in_specs,
        out_specs=out_specs,
        out_shape=out_shape,
        compiler_params=_cparams(1),
        name="mla_pre",
    )(*args)


def _cache_kv_kernel(lat_ref, kr_ref, w_kvb_ref, w_kvbt_ref, kn_ref, k_ref, vt_ref):
    gains = _head_gains(kn_ref[0:1, :], None, None)
    _keys_values(lat_ref[...].astype(BF16), kr_ref[...], None, w_kvb_ref, w_kvbt_ref, gains, k_ref, vt_ref)


def _cache_kv(lat, krope_blk, w_kvb, w_kvbt, k_norm, *, tm):
    t = lat.shape[0]
    const = lambda i: (0, 0)
    tok = lambda w: pl.BlockSpec((tm, w), lambda i: (i, 0))
    return pl.pallas_call(
        _cache_kv_kernel,
        grid=(t // tm,),
        in_specs=[tok(KV_LORA), tok(LANES), pl.BlockSpec((KV_LORA, HEAD_W), const),
                  pl.BlockSpec((HEAD_W, KV_LORA), const), pl.BlockSpec((2, LANES), const)],
        out_specs=[tok(HEAD_W), pl.BlockSpec((HEAD_W, tm), lambda i: (0, i))],
        out_shape=[jax.ShapeDtypeStruct((t, HEAD_W), BF16), jax.ShapeDtypeStruct((HEAD_W, t), BF16)],
        compiler_params=_cparams(1),
        name="cache_kv",
    )(lat, krope_blk, w_kvb, w_kvbt, k_norm)


def _sublane_groups(x):
    return x.reshape(x.shape[0] // SUBLANES, SUBLANES, x.shape[1])


def _attn_ctx_kernel(q_ref, k_ref, vt_ref, o_ref):
    heads = [slice(h * LANES, (h + 1) * LANES) for h in range(MLA_HEADS)]
    st = jnp.concatenate([_dot_nt(k_ref[:, blk], q_ref[:, blk]) for blk in heads], axis=1)
    pt = jnp.exp2(st - jnp.max(st, axis=0, keepdims=True))
    inv = 1.0 / jnp.sum(pt, axis=0, keepdims=True)
    pt = pt.astype(BF16)
    seq = q_ref.shape[0]
    outs = []
    for h in range(MLA_HEADS):
        cols = slice(h * seq, (h + 1) * seq)
        v_rows = slice(h * LANES + QK_NOPE, (h + 1) * LANES)
        outs.append(_dot(vt_ref[v_rows, :], pt[:, cols]) * inv[:, cols])
    for pair in range(MLA_HEADS // 2):
        both = jnp.concatenate(outs[2 * pair:2 * pair + 2], axis=0)
        o_ref[:, pair * LANES:(pair + 1) * LANES] = both.T.astype(BF16)


def _attn_ctx(q, k, vt, *, seq):
    t = q.shape[0]
    spec = pl.BlockSpec((seq, HEAD_W), lambda b: (b, 0))
    return pl.pallas_call(
        _attn_ctx_kernel,
        grid=(t // seq,),
        in_specs=[spec, spec, pl.BlockSpec((HEAD_W, seq), lambda b: (0, b))],
        out_specs=pl.BlockSpec((seq, VT_W), lambda b: (b, 0)),
        out_shape=jax.ShapeDtypeStruct((t, VT_W), BF16),
        compiler_params=_cparams(1),
        name="attn_ctx",
    )(q, k, vt)


def _attn_dec_kernel(q_ref, k_ref, vt_ref, kc_ref, vct_ref, o_ref, *scratch, tq, ck):
    seq, past = k_ref.shape[0], kc_ref.shape[0]
    n = seq // tq
    depth = (len(scratch) - 1) // 2
    lag = depth - 1
    bufs = list(zip(scratch[:depth], scratch[depth:2 * depth]))
    first_out = scratch[2 * depth]

    def tile_rows(i):
        return pl.ds(pl.multiple_of(i * tq, tq), tq)

    def tile(head, i, r):
        return (head, i) + bufs[(head * n + r) % depth]

    def step(nxt, cur):
        if nxt is not None:
            nblk = slice(nxt[0] * LANES, (nxt[0] + 1) * LANES)
            q = q_ref[tile_rows(nxt[1]), nblk]
            mx = None
        if cur is not None:
            cblk = slice(cur[0] * LANES, (cur[0] + 1) * LANES)
            m = cur[3][...]
            l = jnp.zeros((SUBLANES, tq), F32)
            acc = jnp.zeros((LANES, tq), F32)
        key_blocks = [(k_ref, vt_ref, c * ck, c * ck, ck) for c in range(seq // ck)]
        key_blocks.append((kc_ref, vct_ref, 0, seq, past))
        for keys, values_t, r0, s0, nr in key_blocks:
            if nxt is not None:
                st = _dot_nt(keys[r0:r0 + nr, nblk], q)
                nxt[2][s0:s0 + nr, :] = st
                blk_max = jnp.max(_sublane_groups(st), axis=0)
                mx = blk_max if mx is None else jnp.maximum(mx, blk_max)
            if cur is not None:
                p = jnp.exp2(_sublane_groups(cur[2][s0:s0 + nr, :]) - m[None])
                l = l + jnp.sum(p, axis=0)
                acc = acc + _dot(values_t[cblk, r0:r0 + nr], p.reshape(nr, tq).astype(BF16))
        if nxt is not None:
            nxt[3][...] = jnp.broadcast_to(jnp.max(mx, axis=0, keepdims=True), (SUBLANES, tq))
        if cur is not None:
            ot = acc[QK_NOPE:, :] * (1.0 / jnp.sum(l, axis=0, keepdims=True))
            if cur[0] == 0:
                first_out[cur[1]] = ot
            else:
                pair = jnp.concatenate([first_out[cur[1]], ot], axis=0)
                o_ref[tile_rows(cur[1]), :] = pair.T.astype(BF16)

    def ahead(head, i, r):
        if isinstance(i, int) and i + lag >= n:
            return tile(head + 1, i + lag - n, r + lag - n) if head == 0 else None
        return tile(head, i + lag, r + lag)

    for i in range(lag):
        step(tile(0, i, i), None)
    full = (n - lag) // depth
    for head in range(2):
        def body(j, carry, head=head):
            for r in range(depth):
                step(ahead(head, depth * j + r, r), tile(head, depth * j + r, r))
            return carry

        lax.fori_loop(0, full, body, 0)
        for i in range(full * depth, n):
            step(ahead(head, i, i), tile(head, i, i))


def _attn_dec(q, k, vt, kc, vct, *, seq, past, tq, ck):
    t = q.shape[0]
    n = seq // tq
    assert seq % tq == 0 and n >= ATTN_TILES_IN_FLIGHT and seq % ck == 0 and MLA_HEADS % 2 == 0
    cur = pl.BlockSpec((seq, 2 * LANES), lambda b, h: (b, h))
    cur_t = pl.BlockSpec((2 * LANES, seq), lambda b, h: (h, b))
    old = pl.BlockSpec((past, 2 * LANES), lambda b, h: (b, h))
    old_t = pl.BlockSpec((2 * LANES, past), lambda b, h: (h, b))
    score_buf = pltpu.VMEM((seq + past, tq), F32)
    max_buf = pltpu.VMEM((SUBLANES, tq), F32)
    return pl.pallas_call(
        functools.partial(_attn_dec_kernel, tq=tq, ck=ck),
        grid=(t // seq, MLA_HEADS // 2),
        in_specs=[cur, cur, cur_t, old, old_t],
        out_specs=pl.BlockSpec((seq, 2 * V_HEAD), lambda b, h: (b, h)),
        out_shape=jax.ShapeDtypeStruct((t, VT_W), BF16),
        scratch_shapes=([score_buf] * ATTN_TILES_IN_FLIGHT + [max_buf] * ATTN_TILES_IN_FLIGHT
                        + [pltpu.VMEM((n, V_HEAD, tq), F32)]),
        compiler_params=_cparams(2),
        name="attn_dec",
    )(q, k, vt, kc, vct)


def _head_blocks(w, width):
    rows = w.shape[0]
    w = w.reshape(rows, MLA_HEADS, width)
    return jnp.pad(w, ((0, 0), (0, 0), (0, LANES - width))).reshape(rows, HEAD_W)


def _rope_partner(v):
    quarter = QK_ROPE // 4
    return v.reshape(v.shape[:-1] + (2, 2, quarter))[..., ::-1, :].reshape(v.shape)


def _rope_lanes(v, fill=0.0):
    pad = [(0, 0)] * (v.ndim - 1) + [(QK_NOPE, LANES - QK_HEAD)]
    return jnp.pad(v, pad, constant_values=fill)


def _mla_weights(j, w_in, q_a_norm, kv_a_norm, w_qb, w_kvb, q_norm, k_norm, gmlp_v_norm, gmlp_ws,
                 gmlp_b, w_out):
    o_kv, o_kr, o_g = Q_LORA, Q_LORA + KV_LORA, Q_LORA + KV_LORA + QK_ROPE
    w_krope = w_in[j][:, o_kr:o_g]
    w_in_p = jnp.concatenate([w_in[j][:, :o_kr], _rope_lanes(w_krope), _rope_lanes(_rope_partner(w_krope)),
                              w_in[j][:, o_g:]], axis=1)
    gain_rows = lambda g: jnp.stack([jnp.pad(g, (0, LANES - QK_HEAD)), _rope_lanes(_rope_partner(g[QK_NOPE:]))])
    w_q_rope = w_qb[j].reshape(Q_LORA, MLA_HEADS, QK_HEAD)[:, :, QK_NOPE:]
    return {
        "w_in": w_in_p.astype(BF16),
        "q_a_norm": q_a_norm[j].reshape(1, Q_LORA),
        "kv_a_norm": kv_a_norm[j].reshape(1, KV_LORA),
        "w_qb": _head_blocks(w_qb[j], QK_HEAD).astype(BF16),
        "w_qb_partner": _rope_lanes(_rope_partner(w_q_rope)).reshape(Q_LORA, HEAD_W).astype(BF16),
        "w_kvb": w_kvb[j].astype(BF16),
        "w_kvbt": w_kvb[j].T.astype(BF16),
        "q_norm": gain_rows(q_norm[j]),
        "k_norm": gain_rows(k_norm[j]),
        "gmlp_v_norm": gmlp_v_norm[j].reshape(1, GMLP_WIDTH),
        "gmlp_ws": gmlp_ws[j].astype(BF16),
        "gmlp_b": jnp.repeat(gmlp_b[j].T, LANES, axis=1),
        "w_out_a": w_out[j][:VT_W].astype(BF16),
        "w_out_g": w_out[j][MLA_HEADS * V_HEAD:].astype(BF16),
    }


def _rope_tables(rows):
    row = jnp.repeat(jnp.arange(rows), GRID_W).astype(F32)
    col = jnp.tile(jnp.arange(GRID_W), rows).astype(F32)
    per_axis = QK_ROPE // 2
    inv = ROPE_BASE ** (-jnp.arange(0, per_axis, 2, dtype=F32) / per_axis)
    ang_r, ang_c = row[:, None] * inv, col[:, None] * inv
    cr, sr, cc, sc = jnp.cos(ang_r), jnp.sin(ang_r), jnp.cos(ang_c), jnp.sin(ang_c)
    cos = _rope_lanes(jnp.concatenate([cr, cr, cc, cc], axis=1), fill=1.0)
    sin = _rope_lanes(jnp.concatenate([-sr, sr, -sc, sc], axis=1))
    return cos, sin


def _trunk(x, mods, gains, ffn_w, mla_w, pool_w, pool_scale, cache, rope, *, seq, mod_row0, mod_seq):
    tm = min(TOKEN_TILE, mod_seq)
    ffn_tm = min(FFN_TILE, mod_seq)
    assert ffn_tm % min(ffn_tm, seq) == 0 and max(ffn_tm, seq) % min(ffn_tm, seq) == 0
    latents = []
    for i in range(DEPTH):
        cond = (i, mod_row0, mod_seq)
        x = _ffn(x, mods, gains, ffn_w, cond=cond, k=0, tm=ffn_tm)
        j = i // 2
        if i % 2 == 0:
            wts = mla_w[j]
            outs = _mla_pre(x, mods, gains, wts, rope, cond=cond, seq=seq, tm=tm, emit_latents=cache is None)
            q, k, vt, gout = outs[:4]
            if cache is None:
                latents.append(outs[4:])
                attn = _attn_ctx(q, k, vt, seq=seq)
            else:
                lat, krope_blk, past = cache
                kc, vct = _cache_kv(lat[j], krope_blk[j], wts["w_kvb"], wts["w_kvbt"], wts["k_norm"], tm=tm)
                attn = _attn_dec(q, k, vt, kc, vct, seq=seq, past=past, tq=ATTN_Q_TILE, ck=ATTN_KEY_CHUNK)
            pre, pre_args = "mix", (attn, gout, wts["w_out_a"], wts["w_out_g"])
        else:
            pre, pre_args = "pool", (pool_w, pool_scale, j)
        x = _ffn(x, mods, gains, ffn_w, cond=cond, k=2, tm=ffn_tm, pre=pre, pre_args=pre_args, seq=seq)
    return x, latents


def kernel(x_prompt, x_sample, cache_ckv, cache_krope, c, c_ctx, w_mod, b_mod, norm_g, ffn_w1, ffn_w3,
           ffn_w2, w_in, q_a_norm, kv_a_norm, w_qb, w_kvb, q_norm, k_norm, gmlp_v_norm, gmlp_ws, gmlp_b,
           w_out, pool_w, pool_scale):
    batch, seq, _ = x_prompt.shape
    dec_batch, dec_seq, _ = x_sample.shape
    past = cache_ckv.shape[2]
    n_mla = w_in.shape[0]

    rows = -(-(1 + dec_batch) // SUBLANES) * SUBLANES
    cvecs = jnp.zeros((rows, D_MODEL), F32).at[0].set(c_ctx).at[1:1 + dec_batch].set(c)
    mods = _modulation(cvecs, w_mod, b_mod).reshape(DEPTH, rows, N_MOD, D_MODEL)

    gains = norm_g.reshape(DEPTH * 3, 1, D_MODEL)
    ffn_w = (ffn_w1.astype(BF16), ffn_w3.astype(BF16), ffn_w2.astype(BF16))
    mla_w = [_mla_weights(j, w_in, q_a_norm, kv_a_norm, w_qb, w_kvb, q_norm, k_norm, gmlp_v_norm,
                          gmlp_ws, gmlp_b, w_out) for j in range(n_mla)]
    pool_wb = pool_w.astype(BF16)
    pool_sc = pool_scale.reshape(-1, 1, D_MODEL)

    y_prompt, latents = _trunk(x_prompt.reshape(batch * seq, D_MODEL), mods, gains, ffn_w, mla_w,
                               pool_wb, pool_sc, None, None, seq=seq, mod_row0=0, mod_seq=batch * seq)
    new_ckv = jnp.stack([l[0].reshape(batch, seq, KV_LORA) for l in latents], axis=1)
    new_krope = jnp.stack(
        [l[1][:, QK_NOPE:QK_HEAD].reshape(batch, seq, QK_ROPE) for l in latents], axis=1)

    cache_lat = cache_ckv.transpose(1, 0, 2, 3).reshape(n_mla, dec_batch * past, KV_LORA)
    cache_kr = cache_krope.transpose(1, 0, 2, 3).reshape(n_mla, dec_batch * past, QK_ROPE)
    cache_kr = jnp.pad(cache_kr, ((0, 0), (0, 0), (QK_NOPE, LANES - QK_HEAD)))
    rope = _rope_tables(dec_seq // GRID_W)
    y_sample, _ = _trunk(x_sample.reshape(dec_batch * dec_seq, D_MODEL), mods, gains, ffn_w, mla_w,
                         pool_wb, pool_sc, (cache_lat, cache_kr, past), rope, seq=dec_seq,
                         mod_row0=1, mod_seq=dec_seq)
    return (y_prompt.reshape(batch, seq, D_MODEL), y_sample.reshape(dec_batch, dec_seq, D_MODEL),
            new_ckv, new_krope)
```

```python
import functools

import jax
import jax.numpy as jnp
from jax import lax
from jax.experimental import pallas as pl
from jax.experimental.pallas import tpu as pltpu

D_MODEL = 1024
DEPTH = 4
N_MOD = 9
FFN_HIDDEN = 2816
EPS = 1e-6
MLA_HEADS = 8
Q_LORA = 256
KV_LORA = 128
QK_NOPE = 64
QK_ROPE = 32
V_HEAD = 64
QK_HEAD = QK_NOPE + QK_ROPE
GRID_W = 64
ROPE_BASE = 10000.0
LOG2_E = 1.4426950408889634
GMLP_GROUPS = 4
GMLP_CHUNK = 128
GMLP_WIDTH = 512
POOL_WINDOWS = (2, 4, 8, 16)
POOL_GROUP_CH = D_MODEL // len(POOL_WINDOWS)
POOL_HALO = max(POOL_WINDOWS) // 2
POOL_BLOCK = 128

LANES = 128
SUBLANES = 8
MXU_TILE = 256
HEAD_W = MLA_HEADS * LANES
VT_W = MLA_HEADS * V_HEAD
PROJ_W = Q_LORA + KV_LORA + 2 * LANES + 2 * GMLP_WIDTH
VMEM_LIMIT = 56 * 1024 * 1024

TOKEN_TILE = 512
FFN_TILE = 1024
ATTN_Q_TILE = 256
ATTN_KEY_CHUNK = 512
ATTN_TILES_IN_FLIGHT = 3

BF16 = jnp.bfloat16
F32 = jnp.float32


def _cparams(n_axes):
    return pltpu.CompilerParams(
        dimension_semantics=("arbitrary",) * n_axes, vmem_limit_bytes=VMEM_LIMIT)


def _rms(x, width):
    ss = jnp.sum(x * x, axis=-1, keepdims=True) * (1.0 / width)
    return x * lax.rsqrt(ss + EPS)


def _modulated(x, g_ref, mod_ref, k):
    shift = mod_ref[0, 3 * k:3 * k + 1, :]
    scale = mod_ref[0, 3 * k + 1:3 * k + 2, :]
    return _rms(x, D_MODEL) * (g_ref[...] * (1.0 + scale)) + shift


def _mod_spec(cond, tm):
    layer, row0, mod_seq = cond
    tiles_per_row = mod_seq // tm
    return pl.BlockSpec((None, 1, N_MOD, D_MODEL), lambda i: (layer, row0 + i // tiles_per_row, 0, 0))


def _gain_spec(layer, k):
    return pl.BlockSpec((None, 1, D_MODEL), lambda i: (3 * layer + k, 0, 0))


def _resident(shape, index):
    return pl.BlockSpec(shape, lambda i: index, pipeline_mode=pl.Buffered(1))


def _dot(a, b):
    return jnp.dot(a, b, preferred_element_type=F32)


def _dot_nt(a, b):
    return lax.dot_general(a, b, (((1,), (1,)), ((), ())), preferred_element_type=F32)


def _mod_kernel(c_ref, w_ref, b_ref, o_ref):
    c = c_ref[...]
    a = (c * (1.0 / (1.0 + jnp.exp(-c)))).astype(BF16)
    o_ref[0] = _dot(a, w_ref[0].astype(BF16)) + b_ref[0]


def _modulation(cvecs, w_mod, b_mod):
    rows = cvecs.shape[0]
    tn = D_MODEL
    n_out = N_MOD * D_MODEL
    return pl.pallas_call(
        _mod_kernel,
        grid=(DEPTH, n_out // tn),
        in_specs=[
            pl.BlockSpec((rows, D_MODEL), lambda i, j: (0, 0)),
            pl.BlockSpec((1, D_MODEL, tn), lambda i, j: (i, 0, j)),
            pl.BlockSpec((1, 1, tn), lambda i, j: (i, 0, j)),
        ],
        out_specs=pl.BlockSpec((1, rows, tn), lambda i, j: (i, 0, j)),
        out_shape=jax.ShapeDtypeStruct((DEPTH, rows, n_out), F32),
        compiler_params=_cparams(2),
        name="modulation",
    )(cvecs, w_mod, b_mod.reshape(DEPTH, 1, n_out))


def _pool_mix(h, prev_h, next_h, pw_ref, ps_ref, hi_buf, lo_buf, *, seq, pos0):
    rows = h.shape[0]
    seg = min(rows, seq)
    whole = seg == seq
    lead = 2 * POOL_HALO
    win = 2 * POOL_BLOCK
    t_idx = lax.broadcasted_iota(jnp.int32, (POOL_BLOCK, win), 0)
    u_idx = lax.broadcasted_iota(jnp.int32, (POOL_BLOCK, win), 1)
    bands = [((u_idx >= t_idx + lead - w // 2) & (u_idx < t_idx + lead + w // 2)).astype(BF16)
             for w in POOL_WINDOWS]
    pos = (0 if whole else pos0) + lax.broadcasted_iota(jnp.int32, (seg, 1), 0)
    inv_counts = [1.0 / (jnp.minimum(pos + w // 2, seq) - jnp.maximum(pos - w // 2, 0)).astype(F32)
                  for w in POOL_WINDOWS]
    pad = jnp.zeros((POOL_HALO, D_MODEL), F32)
    tail = jnp.zeros((POOL_BLOCK - lead - POOL_HALO, D_MODEL), F32)
    mixes = []
    for s in range(rows // seg):
        hs = h[s * seg:(s + 1) * seg, :]
        ext = jnp.concatenate([pad, pad if whole else prev_h, hs, pad if whole else next_h, tail], axis=0)
        hi = ext.astype(BF16)
        hi_buf[...] = hi
        lo_buf[...] = (ext - hi.astype(F32)).astype(BF16)
        outs = []
        for gi in range(len(POOL_WINDOWS)):
            cols = slice(gi * POOL_GROUP_CH, (gi + 1) * POOL_GROUP_CH)
            diffs = []
            for b in range(seg // POOL_BLOCK):
                out_rows = slice(b * POOL_BLOCK, (b + 1) * POOL_BLOCK)
                win_rows = slice(b * POOL_BLOCK, b * POOL_BLOCK + win)
                total = _dot(bands[gi], hi_buf[win_rows, cols]) + _dot(bands[gi], lo_buf[win_rows, cols])
                mean = total * inv_counts[gi][out_rows, :]
                diffs.append((mean - hs[out_rows, cols]).astype(BF16))
            outs.append(_dot(jnp.concatenate(diffs, axis=0), pw_ref[gi]))
        mixes.append(jnp.concatenate(outs, axis=-1))
    mix = mixes[0] if len(mixes) == 1 else jnp.concatenate(mixes, axis=0)
    return mix * ps_ref[...]


def _ffn_kernel(*refs, k, pre, seq):
    refs = list(refs)
    x_ref = refs.pop(0)
    if pre == "mix":
        attn_ref, gout_ref, wa_ref, wg_ref = refs[:4]
        refs = refs[4:]
    elif pre == "pool":
        xp_ref, xn_ref, gp_ref, pw_ref, ps_ref = refs[:5]
        refs = refs[5:]
    mod_ref, g_ref, w1_ref, w3_ref, w2_ref, o_ref, gated_ref = refs[:7]

    x = x_ref[...]
    if pre == "mix":
        mix = _dot(attn_ref[...], wa_ref[...]) + _dot(gout_ref[...], wg_ref[...])
        x = x + mod_ref[0, 5:6, :] * mix
    elif pre == "pool":
        tm = x.shape[0]
        tiles = max(seq // tm, 1)
        j = pl.program_id(0) % tiles
        h = _modulated(x, gp_ref, mod_ref, 1)
        prev_h = _modulated(xp_ref[...], gp_ref, mod_ref, 1) * (j != 0).astype(F32)
        next_h = _modulated(xn_ref[...], gp_ref, mod_ref, 1) * (j != tiles - 1).astype(F32)
        mix = _pool_mix(h, prev_h, next_h, pw_ref, ps_ref, refs[7], refs[8], seq=seq, pos0=j * tm)
        x = x + mod_ref[0, 5:6, :] * mix

    hb = _modulated(x, g_ref, mod_ref, k).astype(BF16)
    for c in range(FFN_HIDDEN // MXU_TILE):
        cols = slice(c * MXU_TILE, (c + 1) * MXU_TILE)
        a = _dot(hb, w1_ref[:, cols])
        b = _dot(hb, w3_ref[:, cols])
        gated_ref[:, cols] = (a * (1.0 / (1.0 + jnp.exp(-a))) * b).astype(BF16)
    gate = mod_ref[0, 3 * k + 2:3 * k + 3, :]
    o_ref[...] = x + (0.5 * gate) * _dot(gated_ref[...], w2_ref[...])


def _ffn(x, mods, gains, ffn_w, *, cond, k, tm, pre=None, pre_args=(), seq=None):
    t = x.shape[0]
    layer = cond[0]
    which = 0 if k == 0 else 1
    tok = lambda w: pl.BlockSpec((tm, w), lambda i: (i, 0))
    in_specs, args, scratch = [tok(D_MODEL)], [x], [pltpu.VMEM((tm, FFN_HIDDEN), BF16)]
    if pre == "mix":
        attn, gout, w_out_a, w_out_g = pre_args
        in_specs += [tok(VT_W), tok(GMLP_WIDTH), _resident((VT_W, D_MODEL), (0, 0)),
                     _resident((GMLP_WIDTH, D_MODEL), (0, 0))]
        args += [attn, gout, w_out_a, w_out_g]
    elif pre == "pool":
        pool_w, pool_scale, j = pre_args
        halo_blocks, n_halo = tm // POOL_HALO, t // POOL_HALO
        in_specs += [
            pl.BlockSpec((POOL_HALO, D_MODEL), lambda i: (jnp.maximum(i * halo_blocks - 1, 0), 0)),
            pl.BlockSpec((POOL_HALO, D_MODEL), lambda i: (jnp.minimum((i + 1) * halo_blocks, n_halo - 1), 0)),
            _gain_spec(layer, 1),
            _resident((None, len(POOL_WINDOWS), POOL_GROUP_CH, POOL_GROUP_CH), (j, 0, 0, 0)),
            _resident((None, 1, D_MODEL), (j, 0, 0)),
        ]
        args += [x, x, gains, pool_w, pool_scale]
        assert min(tm, seq) % POOL_BLOCK == 0
        scratch += [pltpu.VMEM((min(tm, seq) + POOL_BLOCK, D_MODEL), BF16)] * 2
    w1, w3, w2 = ffn_w
    in_specs += [
        _mod_spec(cond, tm),
        _gain_spec(layer, k),
        _resident((None, None, D_MODEL, FFN_HIDDEN), (layer, which, 0, 0)),
        _resident((None, None, D_MODEL, FFN_HIDDEN), (layer, which, 0, 0)),
        _resident((None, None, FFN_HIDDEN, D_MODEL), (layer, which, 0, 0)),
    ]
    args += [mods, gains, w1, w3, w2]
    return pl.pallas_call(
        functools.partial(_ffn_kernel, k=k, pre=pre, seq=seq),
        grid=(t // tm,),
        in_specs=in_specs,
        out_specs=tok(D_MODEL),
        out_shape=jax.ShapeDtypeStruct(x.shape, F32),
        scratch_shapes=scratch,
        compiler_params=_cparams(1),
        name="ffn" if pre is None else pre + "_ffn",
    )(*args)


def _head_gains(gain, gain_partner, rope):
    if rope is None:
        return gain, None
    cos, sin = rope
    return gain * cos, gain_partner * sin


def _norm_rope(x, partner, gains):
    ss = jnp.sum(x * x, axis=-1, keepdims=True) * (1.0 / QK_HEAD)
    y = x * gains[0]
    if gains[1] is not None:
        y = y + partner * gains[1]
    return y * lax.rsqrt(ss + EPS)


def _keys_values(kv_lat, krope_blk, krope_partner, w_kvb_ref, w_kvbt_ref, gains, k_ref, vt_ref):
    vt_ref[...] = _dot_nt(w_kvbt_ref[...], kv_lat).astype(BF16)
    kv = _dot(kv_lat, w_kvb_ref[...])
    nope_lane = lax.broadcasted_iota(jnp.int32, (1, LANES), 1) < QK_NOPE
    for h in range(MLA_HEADS):
        blk = slice(h * LANES, (h + 1) * LANES)
        kh = jnp.where(nope_lane, kv[:, blk], krope_blk)
        k_ref[:, blk] = _norm_rope(kh, krope_partner, gains).astype(BF16)


def _gelu_tanh(x):
    return 0.5 * x * (1.0 + jnp.tanh(0.7978845608028654 * (x + 0.044715 * (x * x * x))))


def _mla_pre_kernel(*refs, roped, emit_latents):
    (x_ref, mod_ref, g_ref, w_in_ref, qa_ref, kva_ref, w_qb_ref, w_kvb_ref, w_kvbt_ref, qn_ref, kn_ref,
     gv_ref, ws_ref, gb_ref) = refs[:14]
    refs = refs[14:]
    rope = None
    if roped:
        w_qbp_ref = refs[0]
        rope = (refs[1][...], refs[2][...])
        refs = refs[3:]
    q_ref, k_ref, vt_ref, gout_ref = refs[:4]

    hb = _modulated(x_ref[...], g_ref, mod_ref, 1).astype(BF16)
    proj = _dot(hb, w_in_ref[...])
    o_kv, o_kr = Q_LORA, Q_LORA + KV_LORA
    o_kp, o_u = o_kr + LANES, o_kr + 2 * LANES
    o_v = o_u + GMLP_WIDTH

    q_lat = (_rms(proj[:, :o_kv], Q_LORA) * qa_ref[...]).astype(BF16)
    q = _dot(q_lat, w_qb_ref[...])
    q_partner = _dot(q_lat, w_qbp_ref[...]) if roped else None
    q_gains = _head_gains(qn_ref[0:1, :] * (QK_HEAD ** -0.5 * LOG2_E),
                          qn_ref[1:2, :] * (QK_HEAD ** -0.5 * LOG2_E), rope)
    for h in range(MLA_HEADS):
        blk = slice(h * LANES, (h + 1) * LANES)
        qp = q_partner[:, blk] if roped else None
        q_ref[:, blk] = _norm_rope(q[:, blk], qp, q_gains).astype(BF16)

    kv_lat = _rms(proj[:, o_kv:o_kr], KV_LORA) * kva_ref[...]
    krope_blk = proj[:, o_kr:o_kp]
    k_gains = _head_gains(kn_ref[0:1, :], kn_ref[1:2, :], rope)
    _keys_values(kv_lat.astype(BF16), krope_blk, proj[:, o_kp:o_u], w_kvb_ref, w_kvbt_ref, k_gains,
                 k_ref, vt_ref)
    if emit_latents:
        refs[4][...] = kv_lat
        refs[5][...] = krope_blk

    u = _gelu_tanh(proj[:, o_u:o_v])
    v = (_rms(_gelu_tanh(proj[:, o_v:]), GMLP_WIDTH) * gv_ref[...]).astype(BF16)
    for n in range(x_ref.shape[0] // GMLP_CHUNK):
        rows = slice(n * GMLP_CHUNK, (n + 1) * GMLP_CHUNK)
        for grp in range(GMLP_GROUPS):
            cols = slice(grp * LANES, (grp + 1) * LANES)
            mixed = _dot(ws_ref[grp], v[rows, cols]) + gb_ref[:, cols]
            gout_ref[rows, cols] = (u[rows, cols] * mixed).astype(BF16)


def _mla_pre(x, mods, gains, wts, rope, *, cond, seq, tm, emit_latents):
    t = x.shape[0]
    per_seq = seq // tm
    const = lambda i: (0, 0)
    tok = lambda w: pl.BlockSpec((tm, w), lambda i: (i, 0))
    in_specs = [
        tok(D_MODEL),
        _mod_spec(cond, tm),
        _gain_spec(cond[0], 1),
        pl.BlockSpec((D_MODEL, PROJ_W), const),
        pl.BlockSpec((1, Q_LORA), const),
        pl.BlockSpec((1, KV_LORA), const),
        pl.BlockSpec((Q_LORA, HEAD_W), const),
        pl.BlockSpec((KV_LORA, HEAD_W), const),
        pl.BlockSpec((HEAD_W, KV_LORA), const),
        pl.BlockSpec((2, LANES), const),
        pl.BlockSpec((2, LANES), const),
        pl.BlockSpec((1, GMLP_WIDTH), const),
        pl.BlockSpec((GMLP_GROUPS, GMLP_CHUNK, GMLP_CHUNK), lambda i: (0, 0, 0)),
        pl.BlockSpec((GMLP_CHUNK, GMLP_WIDTH), const),
    ]
    args = [x, mods, gains, wts["w_in"], wts["q_a_norm"], wts["kv_a_norm"],
            wts["w_qb"], wts["w_kvb"], wts["w_kvbt"], wts["q_norm"], wts["k_norm"], wts["gmlp_v_norm"],
            wts["gmlp_ws"], wts["gmlp_b"]]
    if rope is not None:
        in_specs += [pl.BlockSpec((Q_LORA, HEAD_W), const)]
        in_specs += [pl.BlockSpec((tm, LANES), lambda i: (i % per_seq, 0))] * 2
        args += [wts["w_qb_partner"]] + list(rope)
    out_specs = [tok(HEAD_W), tok(HEAD_W), pl.BlockSpec((HEAD_W, tm), lambda i: (0, i)), tok(GMLP_WIDTH)]
    out_shape = [jax.ShapeDtypeStruct((t, HEAD_W), BF16)] * 2 + [
        jax.ShapeDtypeStruct((HEAD_W, t), BF16), jax.ShapeDtypeStruct((t, GMLP_WIDTH), BF16)]
    if emit_latents:
        out_specs += [tok(KV_LORA), tok(LANES)]
        out_shape += [jax.ShapeDtypeStruct((t, KV_LORA), F32), jax.ShapeDtypeStruct((t, LANES), F32)]
    return pl.pallas_call(
        functools.partial(_mla_pre_kernel, roped=rope is not None, emit_latents=emit_latents),
        grid=(t // tm,),
        in_specs=in_specs,
        out_specs=out_specs,
        out_shape=out_shape,
        compiler_params=_cparams(1),
        name="mla_pre",
    )(*args)


def _cache_kv_kernel(lat_ref, kr_ref, w_kvb_ref, w_kvbt_ref, kn_ref, k_ref, vt_ref):
    gains = _head_gains(kn_ref[0:1, :], None, None)
    _keys_values(lat_ref[...].astype(BF16), kr_ref[...], None, w_kvb_ref, w_kvbt_ref, gains, k_ref, vt_ref)


def _cache_kv(lat, krope_blk, w_kvb, w_kvbt, k_norm, *, tm):
    t = lat.shape[0]
    const = lambda i: (0, 0)
    tok = lambda w: pl.BlockSpec((tm, w), lambda i: (i, 0))
    return pl.pallas_call(
        _cache_kv_kernel,
        grid=(t // tm,),
        in_specs=[tok(KV_LORA), tok(LANES), pl.BlockSpec((KV_LORA, HEAD_W), const),
                  pl.BlockSpec((HEAD_W, KV_LORA), const), pl.BlockSpec((2, LANES), const)],
        out_specs=[tok(HEAD_W), pl.BlockSpec((HEAD_W, tm), lambda i: (0, i))],
        out_shape=[jax.ShapeDtypeStruct((t, HEAD_W), BF16), jax.ShapeDtypeStruct((HEAD_W, t), BF16)],
        compiler_params=_cparams(1),
        name="cache_kv",
    )(lat, krope_blk, w_kvb, w_kvbt, k_norm)


def _sublane_groups(x):
    return x.reshape(x.shape[0] // SUBLANES, SUBLANES, x.shape[1])


def _attn_ctx_kernel(q_ref, k_ref, vt_ref, o_ref):
    heads = [slice(h * LANES, (h + 1) * LANES) for h in range(MLA_HEADS)]
    st = jnp.concatenate([_dot_nt(k_ref[:, blk], q_ref[:, blk]) for blk in heads], axis=1)
    pt = jnp.exp2(st - jnp.max(st, axis=0, keepdims=True))
    inv = 1.0 / jnp.sum(pt, axis=0, keepdims=True)
    pt = pt.astype(BF16)
    seq = q_ref.shape[0]
    outs = []
    for h in range(MLA_HEADS):
        cols = slice(h * seq, (h + 1) * seq)
        v_rows = slice(h * LANES + QK_NOPE, (h + 1) * LANES)
        outs.append(_dot(vt_ref[v_rows, :], pt[:, cols]) * inv[:, cols])
    for pair in range(MLA_HEADS // 2):
        both = jnp.concatenate(outs[2 * pair:2 * pair + 2], axis=0)
        o_ref[:, pair * LANES:(pair + 1) * LANES] = both.T.astype(BF16)


def _attn_ctx(q, k, vt, *, seq):
    t = q.shape[0]
    spec = pl.BlockSpec((seq, HEAD_W), lambda b: (b, 0))
    return pl.pallas_call(
        _attn_ctx_kernel,
        grid=(t // seq,),
        in_specs=[spec, spec, pl.BlockSpec((HEAD_W, seq), lambda b: (0, b))],
        out_specs=pl.BlockSpec((seq, VT_W), lambda b: (b, 0)),
        out_shape=jax.ShapeDtypeStruct((t, VT_W), BF16),
        compiler_params=_cparams(1),
        name="attn_ctx",
    )(q, k, vt)


def _attn_dec_kernel(q_ref, k_ref, vt_ref, kc_ref, vct_ref, o_ref, *scratch, tq, ck):
    seq, past = k_ref.shape[0], kc_ref.shape[0]
    n = seq // tq
    depth = (len(scratch) - 1) // 2
    lag = depth - 1
    bufs = list(zip(scratch[:depth], scratch[depth:2 * depth]))
    first_out = scratch[2 * depth]

    def tile_rows(i):
        return pl.ds(pl.multiple_of(i * tq, tq), tq)

    def tile(head, i, r):
        return (head, i) + bufs[(head * n + r) % depth]

    def step(nxt, cur):
        if nxt is not None:
            nblk = slice(nxt[0] * LANES, (nxt[0] + 1) * LANES)
            q = q_ref[tile_rows(nxt[1]), nblk]
            mx = None
        if cur is not None:
            cblk = slice(cur[0] * LANES + QK_NOPE, (cur[0] + 1) * LANES)
            m = cur[3][...]
            l = jnp.zeros((SUBLANES, tq), F32)
            acc = jnp.zeros((V_HEAD, tq), F32)
        key_blocks = [(k_ref, vt_ref, c * ck, c * ck, ck) for c in range(seq // ck)]
        key_blocks.append((kc_ref, vct_ref, 0, seq, past))
        for keys, values_t, r0, s0, nr in key_blocks:
            if nxt is not None:
                st = _dot_nt(keys[r0:r0 + nr, nblk], q)
                nxt[2][s0:s0 + nr, :] = st
                blk_max = jnp.max(_sublane_groups(st), axis=0)
                mx = blk_max if mx is None else jnp.maximum(mx, blk_max)
            if cur is not None:
                p = jnp.exp2(_sublane_groups(cur[2][s0:s0 + nr, :]) - m[None])
                l = l + jnp.sum(p, axis=0)
                acc = acc + _dot(values_t[cblk, r0:r0 + nr], p.reshape(nr, tq).astype(BF16))
        if nxt is not None:
            nxt[3][...] = jnp.broadcast_to(jnp.max(mx, axis=0, keepdims=True), (SUBLANES, tq))
        if cur is not None:
            ot = acc * (1.0 / jnp.sum(l, axis=0, keepdims=True))
            if cur[0] == 0:
                first_out[cur[1]] = ot
            else:
                pair = jnp.concatenate([first_out[cur[1]], ot], axis=0)
                o_ref[tile_rows(cur[1]), :] = pair.T.astype(BF16)

    def ahead(head, i, r):
        if isinstance(i, int) and i + lag >= n:
            return tile(head + 1, i + lag - n, r + lag - n) if head == 0 else None
        return tile(head, i + lag, r + lag)

    for i in range(lag):
        step(tile(0, i, i), None)
    full = (n - lag) // depth
    for head in range(2):
        def body(j, carry, head=head):
            for r in range(depth):
                step(ahead(head, depth * j + r, r), tile(head, depth * j + r, r))
            return carry

        lax.fori_loop(0, full, body, 0)
        for i in range(full * depth, n):
            step(ahead(head, i, i), tile(head, i, i))


def _attn_dec(q, k, vt, kc, vct, *, seq, past, tq, ck):
    t = q.shape[0]
    n = seq // tq
    assert seq % tq == 0 and n >= ATTN_TILES_IN_FLIGHT and seq % ck == 0 and MLA_HEADS % 2 == 0
    cur = pl.BlockSpec((seq, 2 * LANES), lambda b, h: (b, h))
    cur_t = pl.BlockSpec((2 * LANES, seq), lambda b, h: (h, b))
    old = pl.BlockSpec((past, 2 * LANES), lambda b, h: (b, h))
    old_t = pl.BlockSpec((2 * LANES, past), lambda b, h: (h, b))
    score_buf = pltpu.VMEM((seq + past, tq), F32)
    max_buf = pltpu.VMEM((SUBLANES, tq), F32)
    return pl.pallas_call(
        functools.partial(_attn_dec_kernel, tq=tq, ck=ck),
        grid=(t // seq, MLA_HEADS // 2),
        in_specs=[cur, cur, cur_t, old, old_t],
        out_specs=pl.BlockSpec((seq, 2 * V_HEAD), lambda b, h: (b, h)),
        out_shape=jax.ShapeDtypeStruct((t, VT_W), BF16),
        scratch_shapes=([score_buf] * ATTN_TILES_IN_FLIGHT + [max_buf] * ATTN_TILES_IN_FLIGHT
                        + [pltpu.VMEM((n, V_HEAD, tq), F32)]),
        compiler_params=_cparams(2),
        name="attn_dec",
    )(q, k, vt, kc, vct)


def _head_blocks(w, width):
    rows = w.shape[0]
    w = w.reshape(rows, MLA_HEADS, width)
    return jnp.pad(w, ((0, 0), (0, 0), (0, LANES - width))).reshape(rows, HEAD_W)


def _rope_partner(v):
    quarter = QK_ROPE // 4
    return v.reshape(v.shape[:-1] + (2, 2, quarter))[..., ::-1, :].reshape(v.shape)


def _rope_lanes(v, fill=0.0):
    pad = [(0, 0)] * (v.ndim - 1) + [(QK_NOPE, LANES - QK_HEAD)]
    return jnp.pad(v, pad, constant_values=fill)


def _mla_weights(j, w_in, q_a_norm, kv_a_norm, w_qb, w_kvb, q_norm, k_norm, gmlp_v_norm, gmlp_ws,
                 gmlp_b, w_out):
    o_kv, o_kr, o_g = Q_LORA, Q_LORA + KV_LORA, Q_LORA + KV_LORA + QK_ROPE
    w_krope = w_in[j][:, o_kr:o_g]
    w_in_p = jnp.concatenate([w_in[j][:, :o_kr], _rope_lanes(w_krope), _rope_lanes(_rope_partner(w_krope)),
                              w_in[j][:, o_g:]], axis=1)
    gain_rows = lambda g: jnp.stack([jnp.pad(g, (0, LANES - QK_HEAD)), _rope_lanes(_rope_partner(g[QK_NOPE:]))])
    w_q_rope = w_qb[j].reshape(Q_LORA, MLA_HEADS, QK_HEAD)[:, :, QK_NOPE:]
    return {
        "w_in": w_in_p.astype(BF16),
        "q_a_norm": q_a_norm[j].reshape(1, Q_LORA),
        "kv_a_norm": kv_a_norm[j].reshape(1, KV_LORA),
        "w_qb": _head_blocks(w_qb[j], QK_HEAD).astype(BF16),
        "w_qb_partner": _rope_lanes(_rope_partner(w_q_rope)).reshape(Q_LORA, HEAD_W).astype(BF16),
        "w_kvb": w_kvb[j].astype(BF16),
        "w_kvbt": w_kvb[j].T.astype(BF16),
        "q_norm": gain_rows(q_norm[j]),
        "k_norm": gain_rows(k_norm[j]),
        "gmlp_v_norm": gmlp_v_norm[j].reshape(1, GMLP_WIDTH),
        "gmlp_ws": gmlp_ws[j].astype(BF16),
        "gmlp_b": jnp.repeat(gmlp_b[j].T, LANES, axis=1),
        "w_out_a": w_out[j][:VT_W].astype(BF16),
        "w_out_g": w_out[j][MLA_HEADS * V_HEAD:].astype(BF16),
    }


def _rope_tables(rows):
    row = jnp.repeat(jnp.arange(rows), GRID_W).astype(F32)
    col = jnp.tile(jnp.arange(GRID_W), rows).astype(F32)
    per_axis = QK_ROPE // 2
    inv = ROPE_BASE ** (-jnp.arange(0, per_axis, 2, dtype=F32) / per_axis)
    ang_r, ang_c = row[:, None] * inv, col[:, None] * inv
    cr, sr, cc, sc = jnp.cos(ang_r), jnp.sin(ang_r), jnp.cos(ang_c), jnp.sin(ang_c)
    cos = _rope_lanes(jnp.concatenate([cr, cr, cc, cc], axis=1), fill=1.0)
    sin = _rope_lanes(jnp.concatenate([-sr, sr, -sc, sc], axis=1))
    return cos, sin


def _trunk(x, mods, gains, ffn_w, mla_w, pool_w, pool_scale, cache, rope, *, seq, mod_row0, mod_seq):
    tm = min(TOKEN_TILE, mod_seq)
    ffn_tm = min(FFN_TILE, mod_seq)
    assert ffn_tm % min(ffn_tm, seq) == 0 and max(ffn_tm, seq) % min(ffn_tm, seq) == 0
    latents = []
    for i in range(DEPTH):
        cond = (i, mod_row0, mod_seq)
        x = _ffn(x, mods, gains, ffn_w, cond=cond, k=0, tm=ffn_tm)
        j = i // 2
        if i % 2 == 0:
            wts = mla_w[j]
            outs = _mla_pre(x, mods, gains, wts, rope, cond=cond, seq=seq, tm=tm, emit_latents=cache is None)
            q, k, vt, gout = outs[:4]
            if cache is None:
                latents.append(outs[4:])
                attn = _attn_ctx(q, k, vt, seq=seq)
            else:
                lat, krope_blk, past = cache
                kc, vct = _cache_kv(lat[j], krope_blk[j], wts["w_kvb"], wts["w_kvbt"], wts["k_norm"], tm=tm)
                attn = _attn_dec(q, k, vt, kc, vct, seq=seq, past=past, tq=ATTN_Q_TILE, ck=ATTN_KEY_CHUNK)
            pre, pre_args = "mix", (attn, gout, wts["w_out_a"], wts["w_out_g"])
        else:
            pre, pre_args = "pool", (pool_w, pool_scale, j)
        x = _ffn(x, mods, gains, ffn_w, cond=cond, k=2, tm=ffn_tm, pre=pre, pre_args=pre_args, seq=seq)
    return x, latents


def kernel(x_prompt, x_sample, cache_ckv, cache_krope, c, c_ctx, w_mod, b_mod, norm_g, ffn_w1, ffn_w3,
           ffn_w2, w_in, q_a_norm, kv_a_norm, w_qb, w_kvb, q_norm, k_norm, gmlp_v_norm, gmlp_ws, gmlp_b,
           w_out, pool_w, pool_scale):
    batch, seq, _ = x_prompt.shape
    dec_batch, dec_seq, _ = x_sample.shape
    past = cache_ckv.shape[2]
    n_mla = w_in.shape[0]

    rows = -(-(1 + dec_batch) // SUBLANES) * SUBLANES
    cvecs = jnp.zeros((rows, D_MODEL), F32).at[0].set(c_ctx).at[1:1 + dec_batch].set(c)
    mods = _modulation(cvecs, w_mod, b_mod).reshape(DEPTH, rows, N_MOD, D_MODEL)

    gains = norm_g.reshape(DEPTH * 3, 1, D_MODEL)
    ffn_w = (ffn_w1.astype(BF16), ffn_w3.astype(BF16), ffn_w2.astype(BF16))
    mla_w = [_mla_weights(j, w_in, q_a_norm, kv_a_norm, w_qb, w_kvb, q_norm, k_norm, gmlp_v_norm,
                          gmlp_ws, gmlp_b, w_out) for j in range(n_mla)]
    pool_wb = pool_w.astype(BF16)
    pool_sc = pool_scale.reshape(-1, 1, D_MODEL)

    y_prompt, latents = _trunk(x_prompt.reshape(batch * seq, D_MODEL), mods, gains, ffn_w, mla_w,
                               pool_wb, pool_sc, None, None, seq=seq, mod_row0=0, mod_seq=batch * seq)
    new_ckv = jnp.stack([l[0].reshape(batch, seq, KV_LORA) for l in latents], axis=1)
    new_krope = jnp.stack(
        [l[1][:, QK_NOPE:QK_HEAD].reshape(batch, seq, QK_ROPE) for l in latents], axis=1)

    cache_lat = cache_ckv.transpose(1, 0, 2, 3).reshape(n_mla, dec_batch * past, KV_LORA)
    cache_kr = cache_krope.transpose(1, 0, 2, 3).reshape(n_mla, dec_batch * past, QK_ROPE)
    cache_kr = jnp.pad(cache_kr, ((0, 0), (0, 0), (QK_NOPE, LANES - QK_HEAD)))
    rope = _rope_tables(dec_seq // GRID_W)
    y_sample, _ = _trunk(x_sample.reshape(dec_batch * dec_seq, D_MODEL), mods, gains, ffn_w, mla_w,
                         pool_wb, pool_sc, (cache_lat, cache_kr, past), rope, seq=dec_seq,
                         mod_row0=1, mod_seq=dec_seq)
    return (y_prompt.reshape(batch, seq, D_MODEL), y_sample.reshape(dec_batch, dec_seq, D_MODEL),
            new_ckv, new_krope)
```

```python
import functools

import jax
import jax.numpy as jnp
from jax import lax
from jax.experimental import pallas as pl
from jax.experimental.pallas import tpu as pltpu

D_MODEL = 1024
DEPTH = 4
N_MOD = 9
FFN_HIDDEN = 2816
EPS = 1e-6
MLA_HEADS = 8
Q_LORA = 256
KV_LORA = 128
QK_NOPE = 64
QK_ROPE = 32
V_HEAD = 64
QK_HEAD = QK_NOPE + QK_ROPE
GRID_W = 64
ROPE_BASE = 10000.0
LOG2_E = 1.4426950408889634
GMLP_GROUPS = 4
GMLP_CHUNK = 128
GMLP_WIDTH = 512
POOL_WINDOWS = (2, 4, 8, 16)
POOL_GROUP_CH = D_MODEL // len(POOL_WINDOWS)
POOL_HALO = max(POOL_WINDOWS) // 2
POOL_BLOCK = 128

LANES = 128
SUBLANES = 8
MXU_TILE = 256
HEAD_W = MLA_HEADS * LANES
VT_W = MLA_HEADS * V_HEAD
PROJ_W = Q_LORA + KV_LORA + 2 * LANES + 2 * GMLP_WIDTH
VMEM_LIMIT = 56 * 1024 * 1024

TOKEN_TILE = 512
FFN_TILE = 1024
ATTN_Q_TILE = 256
ATTN_KEY_CHUNK = 512
ATTN_TILES_IN_FLIGHT = 3

BF16 = jnp.bfloat16
F32 = jnp.float32


def _cparams(n_axes):
    return pltpu.CompilerParams(
        dimension_semantics=("arbitrary",) * n_axes, vmem_limit_bytes=VMEM_LIMIT)


def _rms(x, width):
    ss = jnp.sum(x * x, axis=-1, keepdims=True) * (1.0 / width)
    return x * lax.rsqrt(ss + EPS)


def _modulated(x, g_ref, mod_ref, k):
    shift = mod_ref[0, 3 * k:3 * k + 1, :]
    scale = mod_ref[0, 3 * k + 1:3 * k + 2, :]
    return _rms(x, D_MODEL) * (g_ref[...] * (1.0 + scale)) + shift


def _mod_spec(cond, tm):
    layer, row0, mod_seq = cond
    tiles_per_row = mod_seq // tm
    return pl.BlockSpec((None, 1, N_MOD, D_MODEL), lambda i: (layer, row0 + i // tiles_per_row, 0, 0))


def _gain_spec(layer, k):
    return pl.BlockSpec((None, 1, D_MODEL), lambda i: (3 * layer + k, 0, 0))


def _resident(shape, index):
    return pl.BlockSpec(shape, lambda i: index, pipeline_mode=pl.Buffered(1))


def _dot(a, b):
    return jnp.dot(a, b, preferred_element_type=F32)


def _dot_nt(a, b):
    return lax.dot_general(a, b, (((1,), (1,)), ((), ())), preferred_element_type=F32)


def _mod_kernel(c_ref, w_ref, b_ref, o_ref):
    c = c_ref[...]
    a = (c * (1.0 / (1.0 + jnp.exp(-c)))).astype(BF16)
    o_ref[0] = _dot(a, w_ref[0].astype(BF16)) + b_ref[0]


def _modulation(cvecs, w_mod, b_mod):
    rows = cvecs.shape[0]
    tn = D_MODEL
    n_out = N_MOD * D_MODEL
    return pl.pallas_call(
        _mod_kernel,
        grid=(DEPTH, n_out // tn),
        in_specs=[
            pl.BlockSpec((rows, D_MODEL), lambda i, j: (0, 0)),
            pl.BlockSpec((1, D_MODEL, tn), lambda i, j: (i, 0, j)),
            pl.BlockSpec((1, 1, tn), lambda i, j: (i, 0, j)),
        ],
        out_specs=pl.BlockSpec((1, rows, tn), lambda i, j: (i, 0, j)),
        out_shape=jax.ShapeDtypeStruct((DEPTH, rows, n_out), F32),
        compiler_params=_cparams(2),
        name="modulation",
    )(cvecs, w_mod, b_mod.reshape(DEPTH, 1, n_out))


def _pool_mix(h, prev_h, next_h, pw_ref, ps_ref, hi_buf, lo_buf, *, seq, pos0):
    rows = h.shape[0]
    seg = min(rows, seq)
    whole = seg == seq
    lead = 2 * POOL_HALO
    win = 2 * POOL_BLOCK
    t_idx = lax.broadcasted_iota(jnp.int32, (POOL_BLOCK, win), 0)
    u_idx = lax.broadcasted_iota(jnp.int32, (POOL_BLOCK, win), 1)
    bands = [((u_idx >= t_idx + lead - w // 2) & (u_idx < t_idx + lead + w // 2)).astype(BF16)
             for w in POOL_WINDOWS]
    pos = (0 if whole else pos0) + lax.broadcasted_iota(jnp.int32, (seg, 1), 0)
    inv_counts = [1.0 / (jnp.minimum(pos + w // 2, seq) - jnp.maximum(pos - w // 2, 0)).astype(F32)
                  for w in POOL_WINDOWS]
    pad = jnp.zeros((POOL_HALO, D_MODEL), F32)
    tail = jnp.zeros((POOL_BLOCK - lead - POOL_HALO, D_MODEL), F32)
    mixes = []
    for s in range(rows // seg):
        hs = h[s * seg:(s + 1) * seg, :]
        ext = jnp.concatenate([pad, pad if whole else prev_h, hs, pad if whole else next_h, tail], axis=0)
        hi = ext.astype(BF16)
        hi_buf[...] = hi
        lo_buf[...] = (ext - hi.astype(F32)).astype(BF16)
        outs = []
        for gi in range(len(POOL_WINDOWS)):
            cols = slice(gi * POOL_GROUP_CH, (gi + 1) * POOL_GROUP_CH)
            diffs = []
            for b in range(seg // POOL_BLOCK):
                out_rows = slice(b * POOL_BLOCK, (b + 1) * POOL_BLOCK)
                win_rows = slice(b * POOL_BLOCK, b * POOL_BLOCK + win)
                total = _dot(bands[gi], hi_buf[win_rows, cols]) + _dot(bands[gi], lo_buf[win_rows, cols])
                mean = total * inv_counts[gi][out_rows, :]
                diffs.append((mean - hs[out_rows, cols]).astype(BF16))
            outs.append(_dot(jnp.concatenate(diffs, axis=0), pw_ref[gi]))
        mixes.append(jnp.concatenate(outs, axis=-1))
    mix = mixes[0] if len(mixes) == 1 else jnp.concatenate(mixes, axis=0)
    return mix * ps_ref[...]


def _ffn_kernel(*refs, k, pre, seq):
    refs = list(refs)
    x_ref = refs.pop(0)
    if pre == "mix":
        attn_ref, gout_ref, wa_ref, wg_ref = refs[:4]
        refs = refs[4:]
    elif pre == "pool":
        xp_ref, xn_ref, gp_ref, pw_ref, ps_ref = refs[:5]
        refs = refs[5:]
    mod_ref, g_ref, w1_ref, w3_ref, w2_ref, o_ref, gated_ref = refs[:7]

    x = x_ref[...]
    if pre == "mix":
        mix = _dot(attn_ref[...], wa_ref[...]) + _dot(gout_ref[...], wg_ref[...])
        x = x + mod_ref[0, 5:6, :] * mix
    elif pre == "pool":
        tm = x.shape[0]
        tiles = max(seq // tm, 1)
        j = pl.program_id(0) % tiles
        h = _modulated(x, gp_ref, mod_ref, 1)
        prev_h = _modulated(xp_ref[...], gp_ref, mod_ref, 1) * (j != 0).astype(F32)
        next_h = _modulated(xn_ref[...], gp_ref, mod_ref, 1) * (j != tiles - 1).astype(F32)
        mix = _pool_mix(h, prev_h, next_h, pw_ref, ps_ref, refs[7], refs[8], seq=seq, pos0=j * tm)
        x = x + mod_ref[0, 5:6, :] * mix

    hb = _modulated(x, g_ref, mod_ref, k).astype(BF16)
    for c in range(FFN_HIDDEN // MXU_TILE):
        cols = slice(c * MXU_TILE, (c + 1) * MXU_TILE)
        a = _dot(hb, w1_ref[:, cols])
        b = _dot(hb, w3_ref[:, cols])
        gated_ref[:, cols] = (a * (1.0 / (1.0 + jnp.exp(-a))) * b).astype(BF16)
    gate = mod_ref[0, 3 * k + 2:3 * k + 3, :]
    o_ref[...] = x + (0.5 * gate) * _dot(gated_ref[...], w2_ref[...])


def _ffn(x, mods, gains, ffn_w, *, cond, k, tm, pre=None, pre_args=(), seq=None):
    t = x.shape[0]
    layer = cond[0]
    which = 0 if k == 0 else 1
    tok = lambda w: pl.BlockSpec((tm, w), lambda i: (i, 0))
    in_specs, args, scratch = [tok(D_MODEL)], [x], [pltpu.VMEM((tm, FFN_HIDDEN), BF16)]
    if pre == "mix":
        attn, gout, w_out_a, w_out_g = pre_args
        in_specs += [tok(VT_W), tok(GMLP_WIDTH), _resident((VT_W, D_MODEL), (0, 0)),
                     _resident((GMLP_WIDTH, D_MODEL), (0, 0))]
        args += [attn, gout, w_out_a, w_out_g]
    elif pre == "pool":
        pool_w, pool_scale, j = pre_args
        halo_blocks, n_halo = tm // POOL_HALO, t // POOL_HALO
        in_specs += [
            pl.BlockSpec((POOL_HALO, D_MODEL), lambda i: (jnp.maximum(i * halo_blocks - 1, 0), 0)),
            pl.BlockSpec((POOL_HALO, D_MODEL), lambda i: (jnp.minimum((i + 1) * halo_blocks, n_halo - 1), 0)),
            _gain_spec(layer, 1),
            _resident((None, len(POOL_WINDOWS), POOL_GROUP_CH, POOL_GROUP_CH), (j, 0, 0, 0)),
            _resident((None, 1, D_MODEL), (j, 0, 0)),
        ]
        args += [x, x, gains, pool_w, pool_scale]
        assert min(tm, seq) % POOL_BLOCK == 0
        scratch += [pltpu.VMEM((min(tm, seq) + POOL_BLOCK, D_MODEL), BF16)] * 2
    w1, w3, w2 = ffn_w
    in_specs += [
        _mod_spec(cond, tm),
        _gain_spec(layer, k),
        _resident((None, None, D_MODEL, FFN_HIDDEN), (layer, which, 0, 0)),
        _resident((None, None, D_MODEL, FFN_HIDDEN), (layer, which, 0, 0)),
        _resident((None, None, FFN_HIDDEN, D_MODEL), (layer, which, 0, 0)),
    ]
    args += [mods, gains, w1, w3, w2]
    return pl.pallas_call(
        functools.partial(_ffn_kernel, k=k, pre=pre, seq=seq),
        grid=(t // tm,),
        in_specs=in_specs,
        out_specs=tok(D_MODEL),
        out_shape=jax.ShapeDtypeStruct(x.shape, F32),
        scratch_shapes=scratch,
        compiler_params=_cparams(1),
        name="ffn" if pre is None else pre + "_ffn",
    )(*args)


def _head_gains(gain, gain_partner, rope):
    if rope is None:
        return gain, None
    cos, sin = rope
    return gain * cos, gain_partner * sin


def _norm_rope(x, partner, gains):
    ss = jnp.sum(x * x, axis=-1, keepdims=True) * (1.0 / QK_HEAD)
    y = x * gains[0]
    if gains[1] is not None:
        y = y + partner * gains[1]
    return y * lax.rsqrt(ss + EPS)


def _keys_values(kv_lat, krope_blk, krope_partner, w_kvb_ref, w_kvbt_ref, gains, k_ref, vt_ref):
    vt_ref[...] = _dot_nt(w_kvbt_ref[...], kv_lat).astype(BF16)
    kv = _dot(kv_lat, w_kvb_ref[...])
    nope_lane = lax.broadcasted_iota(jnp.int32, (1, LANES), 1) < QK_NOPE
    for h in range(MLA_HEADS):
        blk = slice(h * LANES, (h + 1) * LANES)
        kh = jnp.where(nope_lane, kv[:, blk], krope_blk)
        k_ref[:, blk] = _norm_rope(kh, krope_partner, gains).astype(BF16)


def _gelu_tanh(x):
    return 0.5 * x * (1.0 + jnp.tanh(0.7978845608028654 * (x + 0.044715 * (x * x * x))))


def _mla_pre_kernel(*refs, roped, emit_latents):
    (x_ref, mod_ref, g_ref, w_in_ref, qa_ref, kva_ref, w_qb_ref, w_kvb_ref, w_kvbt_ref, qn_ref, kn_ref,
     gv_ref, ws_ref, gb_ref) = refs[:14]
    refs = refs[14:]
    rope = None
    if roped:
        w_qbp_ref = refs[0]
        rope = (refs[1][...], refs[2][...])
        refs = refs[3:]
    q_ref, k_ref, vt_ref, gout_ref = refs[:4]

    hb = _modulated(x_ref[...], g_ref, mod_ref, 1).astype(BF16)
    proj = _dot(hb, w_in_ref[...])
    o_kv, o_kr = Q_LORA, Q_LORA + KV_LORA
    o_kp, o_u = o_kr + LANES, o_kr + 2 * LANES
    o_v = o_u + GMLP_WIDTH

    q_lat = (_rms(proj[:, :o_kv], Q_LORA) * qa_ref[...]).astype(BF16)
    q = _dot(q_lat, w_qb_ref[...])
    q_partner = _dot(q_lat, w_qbp_ref[...]) if roped else None
    q_gains = _head_gains(qn_ref[0:1, :] * (QK_HEAD ** -0.5 * LOG2_E),
                          qn_ref[1:2, :] * (QK_HEAD ** -0.5 * LOG2_E), rope)
    for h in range(MLA_HEADS):
        blk = slice(h * LANES, (h + 1) * LANES)
        qp = q_partner[:, blk] if roped else None
        q_ref[:, blk] = _norm_rope(q[:, blk], qp, q_gains).astype(BF16)

    kv_lat = _rms(proj[:, o_kv:o_kr], KV_LORA) * kva_ref[...]
    krope_blk = proj[:, o_kr:o_kp]
    k_gains = _head_gains(kn_ref[0:1, :], kn_ref[1:2, :], rope)
    _keys_values(kv_lat.astype(BF16), krope_blk, proj[:, o_kp:o_u], w_kvb_ref, w_kvbt_ref, k_gains,
                 k_ref, vt_ref)
    if emit_latents:
        refs[4][...] = kv_lat
        refs[5][...] = krope_blk

    u = _gelu_tanh(proj[:, o_u:o_v])
    v = (_rms(_gelu_tanh(proj[:, o_v:]), GMLP_WIDTH) * gv_ref[...]).astype(BF16)
    for n in range(x_ref.shape[0] // GMLP_CHUNK):
        rows = slice(n * GMLP_CHUNK, (n + 1) * GMLP_CHUNK)
        for grp in range(GMLP_GROUPS):
            cols = slice(grp * LANES, (grp + 1) * LANES)
            mixed = _dot(ws_ref[grp], v[rows, cols]) + gb_ref[:, cols]
            gout_ref[rows, cols] = (u[rows, cols] * mixed).astype(BF16)


def _mla_pre(x, mods, gains, wts, rope, *, cond, seq, tm, emit_latents):
    t = x.shape[0]
    per_seq = seq // tm
    const = lambda i: (0, 0)
    tok = lambda w: pl.BlockSpec((tm, w), lambda i: (i, 0))
    in_specs = [
        tok(D_MODEL),
        _mod_spec(cond, tm),
        _gain_spec(cond[0], 1),
        pl.BlockSpec((D_MODEL, PROJ_W), const),
        pl.BlockSpec((1, Q_LORA), const),
        pl.BlockSpec((1, KV_LORA), const),
        pl.BlockSpec((Q_LORA, HEAD_W), const),
        pl.BlockSpec((KV_LORA, HEAD_W), const),
        pl.BlockSpec((HEAD_W, KV_LORA), const),
        pl.BlockSpec((2, LANES), const),
        pl.BlockSpec((2, LANES), const),
        pl.BlockSpec((1, GMLP_WIDTH), const),
        pl.BlockSpec((GMLP_GROUPS, GMLP_CHUNK, GMLP_CHUNK), lambda i: (0, 0, 0)),
        pl.BlockSpec((GMLP_CHUNK, GMLP_WIDTH), const),
    ]
    args = [x, mods, gains, wts["w_in"], wts["q_a_norm"], wts["kv_a_norm"],
            wts["w_qb"], wts["w_kvb"], wts["w_kvbt"], wts["q_norm"], wts["k_norm"], wts["gmlp_v_norm"],
            wts["gmlp_ws"], wts["gmlp_b"]]
    if rope is not None:
        in_specs += [pl.BlockSpec((Q_LORA, HEAD_W), const)]
        in_specs += [pl.BlockSpec((tm, LANES), lambda i: (i % per_seq, 0))] * 2
        args += [wts["w_qb_partner"]] + list(rope)
    out_specs = [tok(HEAD_W), tok(HEAD_W), pl.BlockSpec((HEAD_W, tm), lambda i: (0, i)), tok(GMLP_WIDTH)]
    out_shape = [jax.ShapeDtypeStruct((t, HEAD_W), BF16)] * 2 + [
        jax.ShapeDtypeStruct((HEAD_W, t), BF16), jax.ShapeDtypeStruct((t, GMLP_WIDTH), BF16)]
    if emit_latents:
        out_specs += [tok(KV_LORA), tok(LANES)]
        out_shape += [jax.ShapeDtypeStruct((t, KV_LORA), F32), jax.ShapeDtypeStruct((t, LANES), F32)]
    return pl.pallas_call(
        functools.partial(_mla_pre_kernel, roped=rope is not None, emit_latents=emit_latents),
        grid=(t // tm,),
        in_specs=in_specs,
        out_specs=out_specs,
        out_shape=out_shape,
        compiler_params=_cparams(1),
        name="mla_pre",
    )(*args)


def _cache_kv_kernel(lat_ref, kr_ref, w_kvb_ref, w_kvbt_ref, kn_ref, k_ref, vt_ref):
    gains = _head_gains(kn_ref[0:1, :], None, None)
    _keys_values(lat_ref[...].astype(BF16), kr_ref[...], None, w_kvb_ref, w_kvbt_ref, gains, k_ref, vt_ref)


def _cache_kv(lat, krope_blk, w_kvb, w_kvbt, k_norm, *, tm):
    t = lat.shape[0]
    const = lambda i: (0, 0)
    tok = lambda w: pl.BlockSpec((tm, w), lambda i: (i, 0))
    return pl.pallas_call(
        _cache_kv_kernel,
        grid=(t // tm,),
        in_specs=[tok(KV_LORA), tok(LANES), pl.BlockSpec((KV_LORA, HEAD_W), const),
                  pl.BlockSpec((HEAD_W, KV_LORA), const), pl.BlockSpec((2, LANES), const)],
        out_specs=[tok(HEAD_W), pl.BlockSpec((HEAD_W, tm), lambda i: (0, i))],
        out_shape=[jax.ShapeDtypeStruct((t, HEAD_W), BF16), jax.ShapeDtypeStruct((HEAD_W, t), BF16)],
        compiler_params=_cparams(1),
        name="cache_kv",
    )(lat, krope_blk, w_kvb, w_kvbt, k_norm)


def _sublane_groups(x):
    return x.reshape(x.shape[0] // SUBLANES, SUBLANES, x.shape[1])


def _attn_ctx_kernel(q_ref, k_ref, vt_ref, o_ref):
    heads = [slice(h * LANES, (h + 1) * LANES) for h in range(MLA_HEADS)]
    st = jnp.concatenate([_dot_nt(k_ref[:, blk], q_ref[:, blk]) for blk in heads], axis=1)
    pt = jnp.exp2(st - jnp.max(st, axis=0, keepdims=True))
    inv = 1.0 / jnp.sum(pt, axis=0, keepdims=True)
    pt = pt.astype(BF16)
    seq = q_ref.shape[0]
    outs = []
    for h in range(MLA_HEADS):
        cols = slice(h * seq, (h + 1) * seq)
        v_rows = slice(h * LANES + QK_NOPE, (h + 1) * LANES)
        outs.append(_dot(vt_ref[v_rows, :], pt[:, cols]) * inv[:, cols])
    for pair in range(MLA_HEADS // 2):
        both = jnp.concatenate(outs[2 * pair:2 * pair + 2], axis=0)
        o_ref[:, pair * LANES:(pair + 1) * LANES] = both.T.astype(BF16)


def _attn_ctx(q, k, vt, *, seq):
    t = q.shape[0]
    spec = pl.BlockSpec((seq, HEAD_W), lambda b: (b, 0))
    return pl.pallas_call(
        _attn_ctx_kernel,
        grid=(t // seq,),
        in_specs=[spec, spec, pl.BlockSpec((HEAD_W, seq), lambda b: (0, b))],
        out_specs=pl.BlockSpec((seq, VT_W), lambda b: (b, 0)),
        out_shape=jax.ShapeDtypeStruct((t, VT_W), BF16),
        compiler_params=_cparams(1),
        name="attn_ctx",
    )(q, k, vt)


def _attn_dec_kernel(q_ref, k_ref, vt_ref, kc_ref, vct_ref, o_ref, *scratch, tq, ck):
    seq, past = k_ref.shape[0], kc_ref.shape[0]
    n = seq // tq
    depth = (len(scratch) - 1) // 2
    lag = depth - 1
    bufs = list(zip(scratch[:depth], scratch[depth:2 * depth]))
    first_out = scratch[2 * depth]

    def tile_rows(i):
        return pl.ds(pl.multiple_of(i * tq, tq), tq)

    def tile(head, i, r):
        return (head, i) + bufs[(head * n + r) % depth]

    def step(nxt, cur):
        if nxt is not None:
            nblk = slice(nxt[0] * LANES, (nxt[0] + 1) * LANES)
            q = q_ref[tile_rows(nxt[1]), nblk]
            mx = None
        if cur is not None:
            cblk = slice(cur[0] * LANES, (cur[0] + 1) * LANES)
            m = cur[3][...]
            l = jnp.zeros((SUBLANES, tq), F32)
            acc = jnp.zeros((LANES, tq), F32)
        key_blocks = [(k_ref, vt_ref, c * ck, c * ck, ck) for c in range(seq // ck)]
        key_blocks.append((kc_ref, vct_ref, 0, seq, past))
        for keys, values_t, r0, s0, nr in key_blocks:
            if nxt is not None:
                st = _dot_nt(keys[r0:r0 + nr, nblk], q)
                nxt[2][s0:s0 + nr, :] = st
                blk_max = jnp.max(_sublane_groups(st), axis=0)
                mx = blk_max if mx is None else jnp.maximum(mx, blk_max)
            if cur is not None:
                p = jnp.exp2(_sublane_groups(cur[2][s0:s0 + nr, :]) - m[None])
                l = l + jnp.sum(p, axis=0)
                acc = acc + _dot(values_t[cblk, r0:r0 + nr], p.reshape(nr, tq).astype(BF16))
        if nxt is not None:
            nxt[3][...] = jnp.broadcast_to(jnp.max(mx, axis=0, keepdims=True), (SUBLANES, tq))
        if cur is not None:
            ot = acc[QK_NOPE:, :] * (1.0 / jnp.sum(l, axis=0, keepdims=True))
            if cur[0] == 0:
                first_out[cur[1]] = ot
            else:
                pair = jnp.concatenate([first_out[cur[1]], ot], axis=0)
                o_ref[tile_rows(cur[1]), :] = pair.T.astype(BF16)

    def ahead(head, i, r):
        if isinstance(i, int) and i + lag >= n:
            return tile(head + 1, i + lag - n, r + lag - n) if head == 0 else None
        return tile(head, i + lag, r + lag)

    for i in range(lag):
        step(tile(0, i, i), None)
    full = (n - lag) // depth
    for head in range(2):
        def body(j, carry, head=head):
            for r in range(depth):
                step(ahead(head, depth * j + r, r), tile(head, depth * j + r, r))
            return carry

        lax.fori_loop(0, full, body, 0)
        for i in range(full * depth, n):
            step(ahead(head, i, i), tile(head, i, i))


def _attn_dec(q, k, vt, kc, vct, *, seq, past, tq, ck):
    t = q.shape[0]
    n = seq // tq
    assert seq % tq == 0 and n >= ATTN_TILES_IN_FLIGHT and seq % ck == 0 and MLA_HEADS % 2 == 0
    cur = pl.BlockSpec((seq, 2 * LANES), lambda b, h: (b, h))
    cur_t = pl.BlockSpec((2 * LANES, seq), lambda b, h: (h, b))
    old = pl.BlockSpec((past, 2 * LANES), lambda b, h: (b, h))
    old_t = pl.BlockSpec((2 * LANES, past), lambda b, h: (h, b))
    score_buf = pltpu.VMEM((seq + past, tq), F32)
    max_buf = pltpu.VMEM((SUBLANES, tq), F32)
    return pl.pallas_call(
        functools.partial(_attn_dec_kernel, tq=tq, ck=ck),
        grid=(t // seq, MLA_HEADS // 2),
        in_specs=[cur, cur, cur_t, old, old_t],
        out_specs=pl.BlockSpec((seq, 2 * V_HEAD), lambda b, h: (b, h)),
        out_shape=jax.ShapeDtypeStruct((t, VT_W), BF16),
        scratch_shapes=([score_buf] * ATTN_TILES_IN_FLIGHT + [max_buf] * ATTN_TILES_IN_FLIGHT
                        + [pltpu.VMEM((n, V_HEAD, tq), F32)]),
        compiler_params=_cparams(2),
        name="attn_dec",
    )(q, k, vt, kc, vct)


def _head_blocks(w, width):
    rows = w.shape[0]
    w = w.reshape(rows, MLA_HEADS, width)
    return jnp.pad(w, ((0, 0), (0, 0), (0, LANES - width))).reshape(rows, HEAD_W)


def _rope_partner(v):
    quarter = QK_ROPE // 4
    return v.reshape(v.shape[:-1] + (2, 2, quarter))[..., ::-1, :].reshape(v.shape)


def _rope_lanes(v, fill=0.0):
    pad = [(0, 0)] * (v.ndim - 1) + [(QK_NOPE, LANES - QK_HEAD)]
    return jnp.pad(v, pad, constant_values=fill)


def _mla_weights(j, w_in, q_a_norm, kv_a_norm, w_qb, w_kvb, q_norm, k_norm, gmlp_v_norm, gmlp_ws,
                 gmlp_b, w_out):
    o_kv, o_kr, o_g = Q_LORA, Q_LORA + KV_LORA, Q_LORA + KV_LORA + QK_ROPE
    w_krope = w_in[j][:, o_kr:o_g]
    w_in_p = jnp.concatenate([w_in[j][:, :o_kr], _rope_lanes(w_krope), _rope_lanes(_rope_partner(w_krope)),
                              w_in[j][:, o_g:]], axis=1)
    gain_rows = lambda g: jnp.stack([jnp.pad(g, (0, LANES - QK_HEAD)), _rope_lanes(_rope_partner(g[QK_NOPE:]))])
    w_q_rope = w_qb[j].reshape(Q_LORA, MLA_HEADS, QK_HEAD)[:, :, QK_NOPE:]
    return {
        "w_in": w_in_p.astype(BF16),
        "q_a_norm": q_a_norm[j].reshape(1, Q_LORA),
        "kv_a_norm": kv_a_norm[j].reshape(1, KV_LORA),
        "w_qb": _head_blocks(w_qb[j], QK_HEAD).astype(BF16),
        "w_qb_partner": _rope_lanes(_rope_partner(w_q_rope)).reshape(Q_LORA, HEAD_W).astype(BF16),
        "w_kvb": w_kvb[j].astype(BF16),
        "w_kvbt": w_kvb[j].T.astype(BF16),
        "q_norm": gain_rows(q_norm[j]),
        "k_norm": gain_rows(k_norm[j]),
        "gmlp_v_norm": gmlp_v_norm[j].reshape(1, GMLP_WIDTH),
        "gmlp_ws": gmlp_ws[j].astype(BF16),
        "gmlp_b": jnp.repeat(gmlp_b[j].T, LANES, axis=1),
        "w_out_a": w_out[j][:VT_W].astype(BF16),
        "w_out_g": w_out[j][MLA_HEADS * V_HEAD:].astype(BF16),
    }


def _rope_tables(rows):
    row = jnp.repeat(jnp.arange(rows), GRID_W).astype(F32)
    col = jnp.tile(jnp.arange(GRID_W), rows).astype(F32)
    per_axis = QK_ROPE // 2
    inv = ROPE_BASE ** (-jnp.arange(0, per_axis, 2, dtype=F32) / per_axis)
    ang_r, ang_c = row[:, None] * inv, col[:, None] * inv
    cr, sr, cc, sc = jnp.cos(ang_r), jnp.sin(ang_r), jnp.cos(ang_c), jnp.sin(ang_c)
    cos = _rope_lanes(jnp.concatenate([cr, cr, cc, cc], axis=1), fill=1.0)
    sin = _rope_lanes(jnp.concatenate([-sr, sr, -sc, sc], axis=1))
    return cos, sin


def _trunk(x, mods, gains, ffn_w, mla_w, pool_w, pool_scale, cache, rope, *, seq, mod_row0, mod_seq):
    tm = min(TOKEN_TILE, mod_seq)
    ffn_tm = min(FFN_TILE, mod_seq)
    assert ffn_tm % min(ffn_tm, seq) == 0 and max(ffn_tm, seq) % min(ffn_tm, seq) == 0
    latents = []
    for i in range(DEPTH):
        cond = (i, mod_row0, mod_seq)
        x = _ffn(x, mods, gains, ffn_w, cond=cond, k=0, tm=ffn_tm)
        j = i // 2
        if i % 2 == 0:
            wts = mla_w[j]
            outs = _mla_pre(x, mods, gains, wts, rope, cond=cond, seq=seq, tm=tm, emit_latents=cache is None)
            q, k, vt, gout = outs[:4]
            if cache is None:
                latents.append(outs[4:])
                attn = _attn_ctx(q, k, vt, seq=seq)
            else:
                lat, krope_blk, past = cache
                kc, vct = _cache_kv(lat[j], krope_blk[j], wts["w_kvb"], wts["w_kvbt"], wts["k_norm"], tm=tm)
                attn = _attn_dec(q, k, vt, kc, vct, seq=seq, past=past, tq=ATTN_Q_TILE, ck=ATTN_KEY_CHUNK)
            pre, pre_args = "mix", (attn, gout, wts["w_out_a"], wts["w_out_g"])
        else:
            pre, pre_args = "pool", (pool_w, pool_scale, j)
        x = _ffn(x, mods, gains, ffn_w, cond=cond, k=2, tm=ffn_tm, pre=pre, pre_args=pre_args, seq=seq)
    return x, latents


def kernel(x_prompt, x_sample, cache_ckv, cache_krope, c, c_ctx, w_mod, b_mod, norm_g, ffn_w1, ffn_w3,
           ffn_w2, w_in, q_a_norm, kv_a_norm, w_qb, w_kvb, q_norm, k_norm, gmlp_v_norm, gmlp_ws, gmlp_b,
           w_out, pool_w, pool_scale):
    batch, seq, _ = x_prompt.shape
    dec_batch, dec_seq, _ = x_sample.shape
    past = cache_ckv.shape[2]
    n_mla = w_in.shape[0]

    rows = -(-(1 + dec_batch) // SUBLANES) * SUBLANES
    cvecs = jnp.zeros((rows, D_MODEL), F32).at[0].set(c_ctx).at[1:1 + dec_batch].set(c)
    mods = _modulation(cvecs, w_mod, b_mod).reshape(DEPTH, rows, N_MOD, D_MODEL)

    gains = norm_g.reshape(DEPTH * 3, 1, D_MODEL)
    ffn_w = (ffn_w1.astype(BF16), ffn_w3.astype(BF16), ffn_w2.astype(BF16))
    mla_w = [_mla_weights(j, w_in, q_a_norm, kv_a_norm, w_qb, w_kvb, q_norm, k_norm, gmlp_v_norm,
                          gmlp_ws, gmlp_b, w_out) for j in range(n_mla)]
    pool_wb = pool_w.astype(BF16)
    pool_sc = pool_scale.reshape(-1, 1, D_MODEL)

    y_prompt, latents = _trunk(x_prompt.reshape(batch * seq, D_MODEL), mods, gains, ffn_w, mla_w,
                               pool_wb, pool_sc, None, None, seq=seq, mod_row0=0, mod_seq=batch * seq)
    new_ckv = jnp.stack([l[0].reshape(batch, seq, KV_LORA) for l in latents], axis=1)
    new_krope = jnp.stack(
        [l[1][:, QK_NOPE:QK_HEAD].reshape(batch, seq, QK_ROPE) for l in latents], axis=1)

    cache_lat = cache_ckv.transpose(1, 0, 2, 3).reshape(n_mla, dec_batch * past, KV_LORA)
    cache_kr = cache_krope.transpose(1, 0, 2, 3).reshape(n_mla, dec_batch * past, QK_ROPE)
    cache_kr = jnp.pad(cache_kr, ((0, 0), (0, 0), (QK_NOPE, LANES - QK_HEAD)))
    rope = _rope_tables(dec_seq // GRID_W)
    y_sample, _ = _trunk(x_sample.reshape(dec_batch * dec_seq, D_MODEL), mods, gains, ffn_w, mla_w,
                         pool_wb, pool_sc, (cache_lat, cache_kr, past), rope, seq=dec_seq,
                         mod_row0=1, mod_seq=dec_seq)
    return (y_prompt.reshape(batch, seq, D_MODEL), y_sample.reshape(dec_batch, dec_seq, D_MODEL),
            new_ckv, new_krope)
```
